```python
import math
import jax, jax.numpy as jnp
from jax import lax
import numpy as np

D_MODEL = 4096
BATCH = 4
SEQ = 2048
DEPTH = 1

GRID_W = 64
CTX_LEN = 256
D_FF = 11008
D_A = 1024
S5_H = 16
S5_G = D_A // S5_H
S5_P = 64
D_B = 2048
HG_DK = 128
HG_HEADS = D_B // HG_DK
HG_DV = D_B // HG_HEADS
CHUNK = 64
N_MOD = 9
N_IN = D_A + 5 * D_B + 2 * D_MODEL
EPS = 1e-6
DT_MIN = 1e-3
DT_MAX = 1e-1

kernel_name = "hybrid_s5_hgrn2_macaron_dit"


def rms_norm(z, gain):
    zf = z.astype(jnp.float32)
    var = jnp.mean(zf * zf, axis=-1, keepdims=True)
    return (zf * lax.rsqrt(var + EPS) * gain.astype(jnp.float32)).astype(z.dtype)


def modulate(z, shift, scale):
    return z * (1.0 + scale) + shift


def swiglu(z, w1, w3, w2):
    return (jax.nn.silu(z @ w1) * (z @ w3)) @ w2


def seg_flip(z, n_ctx):
    return jnp.concatenate([z[:, :n_ctx][:, ::-1], z[:, n_ctx:][:, ::-1]], axis=1)


def to_colmajor(z, rows):
    b, rest = z.shape[0], z.shape[2:]
    return z.reshape((b, rows, GRID_W) + rest).swapaxes(1, 2).reshape((b, rows * GRID_W) + rest)


def from_colmajor(z, rows):
    b, rest = z.shape[0], z.shape[2:]
    return z.reshape((b, GRID_W, rows) + rest).swapaxes(1, 2).reshape((b, rows * GRID_W) + rest)


def _complex_affine_combine(e1, e2):
    a1r, a1i, b1r, b1i = e1
    a2r, a2i, b2r, b2i = e2
    ar = a1r * a2r - a1i * a2i
    ai = a1r * a2i + a1i * a2r
    br = a2r * b1r - a2i * b1i + b2r
    bi = a2r * b1i + a2i * b1r + b2i
    return ar, ai, br, bi


def s5_direction(bu_re, bu_im, lam_re, lam_im, log_dt):
    dt = jnp.exp(log_dt.astype(jnp.float32))[:, None]
    lr, li = lam_re.astype(jnp.float32), lam_im.astype(jnp.float32)
    mag = jnp.exp(lr * dt)
    ab_re, ab_im = mag * jnp.cos(li * dt), mag * jnp.sin(li * dt)
    den = lr * lr + li * li
    nr = ab_re - 1.0
    kr = (nr * lr + ab_im * li) / den
    ki = (ab_im * lr - nr * li) / den
    e_re = kr * bu_re - ki * bu_im
    e_im = kr * bu_im + ki * bu_re
    t_len = bu_re.shape[1]
    a_re = jnp.broadcast_to(ab_re, (1, t_len) + ab_re.shape)
    a_im = jnp.broadcast_to(ab_im, (1, t_len) + ab_im.shape)
    _, _, s_re, s_im = lax.associative_scan(_complex_affine_combine, (a_re, a_im, e_re, e_im), axis=1)
    return s_re, s_im


def s5_mixer(u, n_ctx, lam_re, lam_im, log_dt, b_re, b_im, c_re, c_im, d_skip, w_glu):
    bsz, t_len, _ = u.shape
    ug = u.reshape(bsz, t_len, S5_G, S5_H).astype(jnp.float32)
    bu_re = jnp.einsum("btgh,gph->btgp", ug, b_re.astype(jnp.float32))
    bu_im = jnp.einsum("btgh,gph->btgp", ug, b_im.astype(jnp.float32))
    sf_re, sf_im = s5_direction(bu_re, bu_im, lam_re[0], lam_im[0], log_dt[0])
    sb_re, sb_im = s5_direction(seg_flip(bu_re, n_ctx), seg_flip(bu_im, n_ctx),
                                lam_re[1], lam_im[1], log_dt[1])
    x_re = sf_re + seg_flip(sb_re, n_ctx)
    x_im = sf_im + seg_flip(sb_im, n_ctx)
    y = (jnp.einsum("btgp,ghp->btgh", x_re, c_re.astype(jnp.float32))
         - jnp.einsum("btgp,ghp->btgh", x_im, c_im.astype(jnp.float32))
         + d_skip.reshape(S5_G, S5_H).astype(jnp.float32) * ug)
    y = jax.nn.gelu(y.reshape(bsz, t_len, D_A))
    return (y * jax.nn.sigmoid(y @ w_glu.astype(jnp.float32))).astype(u.dtype)


def hgrn2_chunk(q, k, v, log_f):
    bsz, t_len, n_h, dk = q.shape
    dv = v.shape[-1]
    n_ch = t_len // CHUNK
    rs = lambda z: z.reshape(bsz, n_ch, CHUNK, n_h, z.shape[-1]).astype(jnp.float32)
    q, k, v, log_f = rs(q), rs(k), rs(v), rs(log_f)
    b = jnp.cumsum(log_f, axis=2)
    ref = b[:, :, CHUNK // 2 - 1:CHUNK // 2]
    b_last = b[:, :, -1:]
    scores = jnp.einsum("bnthk,bnshk->bnhts", q * jnp.exp(b - ref), k * jnp.exp(ref - b))
    mask = jnp.tril(jnp.ones((CHUNK, CHUNK), dtype=bool))
    scores = jnp.where(mask, scores, 0.0)
    o_intra = jnp.einsum("bnhts,bnshv->bnthv", scores, v)
    kv = jnp.einsum("bnshk,bnshv->bnhkv", k * jnp.exp(b_last - b), v)
    decay = jnp.exp(b_last[:, :, 0])

    def step(state, inp):
        dec, kv_n = inp
        return dec[..., None] * state + kv_n, state

    init = jnp.zeros((bsz, n_h, dk, dv), jnp.float32)
    _, s_prev = lax.scan(step, init, (jnp.moveaxis(decay, 1, 0), jnp.moveaxis(kv, 1, 0)))
    s_prev = jnp.moveaxis(s_prev, 0, 1)
    o_inter = jnp.einsum("bnthk,bnhkv->bnthv", q * jnp.exp(b), s_prev)
    return (o_intra + o_inter).reshape(bsz, t_len, n_h, dv)


def hgrn2_mixer(q, v, f_pre_fwd, f_pre_bwd, og, n_ctx, rows, lb, g_norm):
    bsz, t_len, _ = q.shape

    def order(z):
        return jnp.concatenate([z[:, :n_ctx], to_colmajor(z[:, n_ctx:], rows)], axis=1)

    def heads(z):
        return z.reshape(bsz, t_len, HG_HEADS, -1)

    q_h, v_h = heads(order(q)), heads(order(v))
    f_fwd = lb[0] + (1.0 - lb[0]) * jax.nn.sigmoid(f_pre_fwd.astype(jnp.float32))
    f_bwd = lb[1] + (1.0 - lb[1]) * jax.nn.sigmoid(f_pre_bwd.astype(jnp.float32))
    f_fwd, f_bwd = heads(order(f_fwd)), heads(order(f_bwd))
    o_fwd = hgrn2_chunk(q_h, 1.0 - f_fwd, v_h, jnp.log(f_fwd))
    o_bwd = hgrn2_chunk(seg_flip(q_h, n_ctx), seg_flip(1.0 - f_bwd, n_ctx),
                        seg_flip(v_h, n_ctx), seg_flip(jnp.log(f_bwd), n_ctx))
    o = o_fwd + seg_flip(o_bwd, n_ctx)
    o = jnp.concatenate([o[:, :n_ctx], from_colmajor(o[:, n_ctx:], rows)], axis=1)
    o = o * lax.rsqrt(jnp.mean(o * o, axis=-1, keepdims=True) + EPS)
    o = o * g_norm.reshape(HG_HEADS, HG_DV).astype(jnp.float32) * jax.nn.silu(heads(og).astype(jnp.float32))
    return o.reshape(bsz, t_len, D_B).astype(q.dtype)


def token_mixer(u_ctx, u_lat, rows, with_ctx_out, w_in, s5_lam_re, s5_lam_im, s5_log_dt,
                s5_b_re, s5_b_im, s5_c_re, s5_c_im, s5_d, s5_w_glu, lb, hg_norm,
                w_proj_a, w_proj_b, w_out):
    n_ctx = u_ctx.shape[1]
    z = jnp.concatenate([u_ctx, u_lat], axis=1) @ w_in
    splits = np.cumsum([D_A, D_B, D_B, D_B, D_B, D_B, D_MODEL]).tolist()
    u_a, q, v, f_fw, f_bw, og, g_a, g_b = jnp.split(z, splits, axis=-1)
    y_a = s5_mixer(u_a, n_ctx, s5_lam_re, s5_lam_im, s5_log_dt, s5_b_re, s5_b_im,
                   s5_c_re, s5_c_im, s5_d, s5_w_glu)
    y_b = hgrn2_mixer(q, v, f_fw, f_bw, og, n_ctx, rows, lb, hg_norm)
    start = 0 if with_ctx_out else n_ctx
    merged = (jax.nn.sigmoid(g_a[:, start:]) * (y_a[:, start:] @ w_proj_a)
              + jax.nn.sigmoid(g_b[:, start:]) * (y_b[:, start:] @ w_proj_b))
    y = merged @ w_out
    if with_ctx_out:
        return y[:, :n_ctx], y[:, n_ctx:]
    return None, y


def setup_inputs(seed: int = 0) -> dict:
    key = jax.random.key(seed)
    ks = iter(jax.random.split(key, 40))

    def nrm(shape, scale):
        return jax.random.normal(next(ks), shape, jnp.float32) * scale

    def gain(shape):
        return 1.0 + nrm(shape, 0.02)

    L = DEPTH
    n_idx = jnp.arange(S5_P, dtype=jnp.float32)
    return {
        "x": nrm((BATCH, SEQ, D_MODEL), 1.0),
        "c": nrm((BATCH, D_MODEL), 1.0),
        "ctx": nrm((BATCH, CTX_LEN, D_MODEL), 1.0),
        "c_ctx": nrm((D_MODEL,), 1.0),
        "w_ada": nrm((L, D_MODEL, N_MOD * D_MODEL), 0.5 * D_MODEL ** -0.5),
        "b_ada": nrm((L, N_MOD * D_MODEL), 0.01),
        "norm_ffn1": gain((L, D_MODEL)),
        "w1_ffn1": nrm((L, D_MODEL, D_FF), D_MODEL ** -0.5),
        "w3_ffn1": nrm((L, D_MODEL, D_FF), D_MODEL ** -0.5),
        "w2_ffn1": nrm((L, D_FF, D_MODEL), D_FF ** -0.5),
        "norm_mix": gain((L, D_MODEL)),
        "w_in": nrm((L, D_MODEL, N_IN), D_MODEL ** -0.5),
        "s5_lam_re": -0.5 + nrm((L, 2, S5_G, S5_P), 0.01),
        "s5_lam_im": math.pi * n_idx + nrm((L, 2, S5_G, S5_P), 0.01),
        "s5_log_dt": jax.random.uniform(next(ks), (L, 2, S5_G), jnp.float32,
                                        math.log(DT_MIN), math.log(DT_MAX)),
        "s5_b_re": nrm((L, S5_G, S5_P, S5_H), (2 * S5_H) ** -0.5),
        "s5_b_im": nrm((L, S5_G, S5_P, S5_H), (2 * S5_H) ** -0.5),
        "s5_c_re": nrm((L, S5_G, S5_H, S5_P), S5_P ** -0.5),
        "s5_c_im": nrm((L, S5_G, S5_H, S5_P), S5_P ** -0.5),
        "s5_d": nrm((L, D_A), 1.0),
        "s5_w_glu": nrm((L, D_A, D_A), D_A ** -0.5),
        "hg_lb_logits": nrm((2, DEPTH + 1, D_B), 0.1),
        "hg_norm": gain((L, D_B)),
        "w_proj_a": nrm((L, D_A, D_MODEL), D_A ** -0.5),
        "w_proj_b": nrm((L, D_B, D_MODEL), D_B ** -0.5),
        "w_out": nrm((L, D_MODEL, D_MODEL), D_MODEL ** -0.5),
        "norm_ffn2": gain((L, D_MODEL)),
        "w1_ffn2": nrm((L, D_MODEL, D_FF), D_MODEL ** -0.5),
        "w3_ffn2": nrm((L, D_MODEL, D_FF), D_MODEL ** -0.5),
        "w2_ffn2": nrm((L, D_FF, D_MODEL), D_FF ** -0.5),
        "norm_final": gain((D_MODEL,)),
    }


def reference(x, c, ctx, c_ctx, w_ada, b_ada, norm_ffn1, w1_ffn1, w3_ffn1, w2_ffn1,
              norm_mix, w_in, s5_lam_re, s5_lam_im, s5_log_dt, s5_b_re, s5_b_im,
              s5_c_re, s5_c_im, s5_d, s5_w_glu, hg_lb_logits, hg_norm, w_proj_a, w_proj_b,
              w_out, norm_ffn2, w1_ffn2, w3_ffn2, w2_ffn2, norm_final):
    rows = x.shape[1] // GRID_W
    lb_table = jnp.cumsum(jax.nn.softmax(hg_lb_logits.astype(jnp.float32), axis=1), axis=1)
    h, hc = x, ctx
    for layer in range(DEPTH):
        last = layer == DEPTH - 1
        mod = jax.nn.silu(c) @ w_ada[layer] + b_ada[layer]
        mod_c = jax.nn.silu(c_ctx) @ w_ada[layer] + b_ada[layer]
        ml = jnp.split(mod[:, None, :], N_MOD, axis=-1)
        mc = jnp.split(mod_c, N_MOD, axis=-1)

        def ffn1(z, m):
            return swiglu(modulate(rms_norm(z, norm_ffn1[layer]), m[0], m[1]),
                          w1_ffn1[layer], w3_ffn1[layer], w2_ffn1[layer])

        def ffn2(z, m):
            return swiglu(modulate(rms_norm(z, norm_ffn2[layer]), m[6], m[7]),
                          w1_ffn2[layer], w3_ffn2[layer], w2_ffn2[layer])

        h = h + 0.5 * ml[2] * ffn1(h, ml)
        hc = hc + 0.5 * mc[2] * ffn1(hc, mc)
        u_lat = modulate(rms_norm(h, norm_mix[layer]), ml[3], ml[4])
        u_ctx = modulate(rms_norm(hc, norm_mix[layer]), mc[3], mc[4])
        y_ctx, y_lat = token_mixer(u_ctx, u_lat, rows, not last, w_in[layer],
                                   s5_lam_re[layer], s5_lam_im[layer], s5_log_dt[layer],
                                   s5_b_re[layer], s5_b_im[layer], s5_c_re[layer], s5_c_im[layer],
                                   s5_d[layer], s5_w_glu[layer], lb_table[:, layer],
                                   hg_norm[layer], w_proj_a[layer], w_proj_b[layer], w_out[layer])
        h = h + ml[5] * y_lat
        h = h + 0.5 * ml[8] * ffn2(h, ml)
        if not last:
            hc = hc + mc[5] * y_ctx
            hc = hc + 0.5 * mc[8] * ffn2(hc, mc)
    return rms_norm(h, norm_final)
```

```python
import functools
import math

import jax
import jax.numpy as jnp
from jax import lax
from jax.experimental import pallas as pl
from jax.experimental.pallas import tpu as pltpu

F32 = jnp.float32
BF16 = jnp.bfloat16
HIGHEST = lax.Precision.HIGHEST

D_MODEL = 4096
BATCH = 4
SEQ = 2048
GRID_W = 64
GRID_ROWS = SEQ // GRID_W
CTX_LEN = 256
T_ALL = CTX_LEN + SEQ
D_FF = 11008
D_FF_PAD = 11264
D_A = 1024
S5_H = 16
S5_G = D_A // S5_H
S5_P = 64
S5_L = 16
D_B = 2048
HG_DK = 128
HG_HEADS = D_B // HG_DK
HG_CHUNK = 64
N_MOD = 9
N_IN = D_A + 5 * D_B + 2 * D_MODEL
EPS = 1e-6

N_CTX_ROWS = BATCH * CTX_LEN
N_LAT_ROWS = BATCH * SEQ
N_ROWS = N_CTX_ROWS + N_LAT_ROWS
MIB = 1024 * 1024


def _params(semantics, vmem_mib):
    return pltpu.CompilerParams(dimension_semantics=semantics, vmem_limit_bytes=vmem_mib * MIB)


def _group_full(tm):
    n_ctx_tiles, per_batch = N_CTX_ROWS // tm, SEQ // tm
    return lambda i: jnp.where(i < n_ctx_tiles, 0, 1 + (i - n_ctx_tiles) // per_batch)


def _group_lat(tm):
    per_batch = SEQ // tm
    return lambda i: 1 + i // per_batch


def _ada_kernel(c_ref, w_ref, b_ref, o_ref):
    sc = jax.nn.silu(c_ref[...]).astype(BF16)
    o_ref[...] = jnp.dot(sc, w_ref[...].astype(BF16), preferred_element_type=F32) + b_ref[...]


def _ada(c8, w_ada, b_ada):
    n = w_ada.shape[1]
    tn = 512
    return pl.pallas_call(
        _ada_kernel,
        grid=(n // tn,),
        in_specs=[pl.BlockSpec((8, D_MODEL), lambda j: (0, 0)),
                  pl.BlockSpec((D_MODEL, tn), lambda j: (0, j)),
                  pl.BlockSpec((1, tn), lambda j: (0, j))],
        out_specs=pl.BlockSpec((8, tn), lambda j: (0, j)),
        out_shape=jax.ShapeDtypeStruct((8, n), F32),
        compiler_params=_params(("parallel",), 40),
        name="ada_table",
    )(c8, w_ada, b_ada.reshape(1, n))


def _norm_mod_kernel(x_ref, g_ref, sh_ref, sc_ref, o_ref):
    x = x_ref[...]
    var = jnp.mean(x * x, axis=-1, keepdims=True)
    y = x * lax.rsqrt(var + EPS) * g_ref[...]
    o_ref[...] = (y * (1.0 + sc_ref[0]) + sh_ref[0]).astype(o_ref.dtype)


def _norm_mod(x, gain, shift, scale, group_fn, tm=512):
    rows = x.shape[0]
    vec = pl.BlockSpec((1, 1, D_MODEL), lambda i: (group_fn(i), 0, 0))
    return pl.pallas_call(
        _norm_mod_kernel,
        grid=(rows // tm,),
        in_specs=[pl.BlockSpec((tm, D_MODEL), lambda i: (i, 0)),
                  pl.BlockSpec((1, D_MODEL), lambda i: (0, 0)),
                  vec, vec],
        out_specs=pl.BlockSpec((tm, D_MODEL), lambda i: (i, 0)),
        out_shape=jax.ShapeDtypeStruct((rows, D_MODEL), BF16),
        compiler_params=_params(("parallel",), 40),
        name="norm_modulate",
    )(x, gain.reshape(1, D_MODEL), shift, scale)


def _final_norm_kernel(x_ref, g_ref, o_ref):
    x = x_ref[...]
    var = jnp.mean(x * x, axis=-1, keepdims=True)
    o_ref[...] = x * lax.rsqrt(var + EPS) * g_ref[...]


def _final_norm(x, gain, tm=512):
    rows = x.shape[0]
    return pl.pallas_call(
        _final_norm_kernel,
        grid=(rows // tm,),
        in_specs=[pl.BlockSpec((tm, D_MODEL), lambda i: (i, 0)),
                  pl.BlockSpec((1, D_MODEL), lambda i: (0, 0))],
        out_specs=pl.BlockSpec((tm, D_MODEL), lambda i: (i, 0)),
        out_shape=jax.ShapeDtypeStruct((rows, D_MODEL), F32),
        compiler_params=_params(("parallel",), 40),
        name="final_norm",
    )(x, gain.reshape(1, D_MODEL))


def _swiglu_up_kernel(z_ref, w1_ref, w3_ref, o_ref):
    z = z_ref[...]
    h1 = jnp.dot(z, w1_ref[...], preferred_element_type=F32)
    h3 = jnp.dot(z, w3_ref[...], preferred_element_type=F32)
    o_ref[...] = (jax.nn.silu(h1) * h3).astype(o_ref.dtype)


def _swiglu_up(z, w1, w3, tm=1024, tn=512):
    m, k = z.shape
    n = w1.shape[1]
    return pl.pallas_call(
        _swiglu_up_kernel,
        grid=(m // tm, n // tn),
        in_specs=[pl.BlockSpec((tm, k), lambda i, j: (i, 0)),
                  pl.BlockSpec((k, tn), lambda i, j: (0, j)),
                  pl.BlockSpec((k, tn), lambda i, j: (0, j))],
        out_specs=pl.BlockSpec((tm, tn), lambda i, j: (i, j)),
        out_shape=jax.ShapeDtypeStruct((m, n), BF16),
        compiler_params=_params(("parallel", "arbitrary"), 52),
        name="swiglu_up",
    )(z, w1, w3)


def _mm_res_kernel(a_ref, w_ref, res_ref, gate_ref, o_ref, acc_ref, *, scale, nk):
    k = pl.program_id(2)

    @pl.when(k == 0)
    def _():
        acc_ref[...] = jnp.zeros_like(acc_ref)

    acc_ref[...] += jnp.dot(a_ref[...], w_ref[...], preferred_element_type=F32)

    @pl.when(k == nk - 1)
    def _():
        o_ref[...] = res_ref[...] + (scale * gate_ref[0]) * acc_ref[...]


def _mm_res(a, w, res, res_tile_off, gate, scale, group_fn, tm, tn, tk, vmem_mib):
    m, k = a.shape
    n = w.shape[1]
    nk = k // tk
    return pl.pallas_call(
        functools.partial(_mm_res_kernel, scale=scale, nk=nk),
        grid=(m // tm, n // tn, nk),
        in_specs=[pl.BlockSpec((tm, tk), lambda i, j, kk: (i, kk)),
                  pl.BlockSpec((tk, tn), lambda i, j, kk: (kk, j)),
                  pl.BlockSpec((tm, tn), lambda i, j, kk: (i + res_tile_off, j)),
                  pl.BlockSpec((1, 1, tn), lambda i, j, kk: (group_fn(i), 0, j))],
        out_specs=pl.BlockSpec((tm, tn), lambda i, j, kk: (i, j)),
        out_shape=jax.ShapeDtypeStruct((m, n), F32),
        scratch_shapes=[pltpu.VMEM((tm, tn), F32)],
        compiler_params=_params(("parallel", "parallel", "arbitrary"), vmem_mib),
        name="matmul_gated_residual",
    )(a, w, res, gate)


def _mm_kernel(a_ref, w_ref, o_ref):
    o_ref[...] = jnp.dot(a_ref[...], w_ref[...], preferred_element_type=F32).astype(o_ref.dtype)


def _mm(a, w, out_dtype, tm, tn, vmem_mib, name):
    m, k = a.shape
    n = w.shape[1]
    return pl.pallas_call(
        _mm_kernel,
        grid=(m // tm, n // tn),
        in_specs=[pl.BlockSpec((tm, k), lambda i, j: (i, 0)),
                  pl.BlockSpec((k, tn), lambda i, j: (0, j))],
        out_specs=pl.BlockSpec((tm, tn), lambda i, j: (i, j)),
        out_shape=jax.ShapeDtypeStruct((m, n), out_dtype),
        compiler_params=_params(("parallel", "arbitrary"), vmem_mib),
        name=name,
    )(a, w)


def _merge_kernel(ya_ref, wa_ref, yb_ref, wb_ref, ga_ref, gb_ref, o_ref):
    pa = jnp.dot(ya_ref[...], wa_ref[...], preferred_element_type=F32)
    pb = jnp.dot(yb_ref[...], wb_ref[...], preferred_element_type=F32)
    o_ref[...] = (jax.nn.sigmoid(ga_ref[...]) * pa + jax.nn.sigmoid(gb_ref[...]) * pb).astype(o_ref.dtype)


def _merge(ya, wa, yb, wb, z, ga_col_tile, gb_col_tile, z_row_tile_off, tm=1024, tn=512):
    m = ya.shape[0]
    n = wa.shape[1]
    return pl.pallas_call(
        _merge_kernel,
        grid=(m // tm, n // tn),
        in_specs=[pl.BlockSpec((tm, D_A), lambda i, j: (i, 0)),
                  pl.BlockSpec((D_A, tn), lambda i, j: (0, j)),
                  pl.BlockSpec((tm, D_B), lambda i, j: (i, 0)),
                  pl.BlockSpec((D_B, tn), lambda i, j: (0, j)),
                  pl.BlockSpec((tm, tn), lambda i, j: (i + z_row_tile_off, ga_col_tile + j)),
                  pl.BlockSpec((tm, tn), lambda i, j: (i + z_row_tile_off, gb_col_tile + j))],
        out_specs=pl.BlockSpec((tm, tn), lambda i, j: (i, j)),
        out_shape=jax.ShapeDtypeStruct((m, n), BF16),
        compiler_params=_params(("parallel", "arbitrary"), 48),
        name="gated_merge",
    )(ya, wa, yb, wb, z, z)


def _s5_prep_kernel(ldt_ref, lam_row_ref, lam_col_ref, b1_ref, b2_ref, c1_ref, c2_ref, dt_ref,
                    wst_ref, wc_ref, a16_ref):
    g = pl.program_id(0)
    lh = S5_L * S5_H
    lane_sign = jnp.where(lax.broadcasted_iota(jnp.int32, (1, 2 * S5_P), 1) < S5_P, -1.0, 1.0)
    row_sign = jnp.where(lax.broadcasted_iota(jnp.int32, (2 * S5_P, 1), 0) < S5_P, 1.0, -1.0)
    log2_h = S5_H.bit_length() - 1
    s_of_row = lax.shift_right_logical(lax.broadcasted_iota(jnp.int32, (lh, 1), 0), log2_h).astype(F32)
    t_of_col = lax.shift_right_logical(lax.broadcasted_iota(jnp.int32, (1, lh), 1), log2_h).astype(F32)
    b1 = b1_ref[0]
    b2 = b2_ref[0]
    c1 = c1_ref[0]
    c2 = c2_ref[0]
    toep = None
    for d in range(2):
        dt = jnp.exp(jnp.full((1, 1), ldt_ref[d, g], F32))
        lr = lam_row_ref[d, 0]
        li = lam_row_ref[2 + d, 0]
        mag = jnp.exp(lr * dt)
        ab_re, ab_im = mag * jnp.cos(li * dt), mag * jnp.sin(li * dt)
        den = lr * lr + li * li
        nr = ab_re - 1.0
        kr = (nr * lr + ab_im * li) / den
        ki = (ab_im * lr - nr * li) / den
        bk = kr * b1 + lane_sign * ki * b2
        bk_sw = lane_sign * (kr * b2 - lane_sign * ki * b1)
        tau = (S5_L - 1.0) - s_of_row if d == 0 else s_of_row
        pm = jnp.exp(tau * (lr * dt))
        p_re, p_im = pm * jnp.cos(tau * (li * dt)), pm * jnp.sin(tau * (li * dt))
        wst_ref[0, :, d * 2 * S5_P:(d + 1) * 2 * S5_P] = p_re * bk + p_im * bk_sw
        m16 = jnp.exp(S5_L * (lr * dt))
        a_re, a_im = m16 * jnp.cos(S5_L * (li * dt)), m16 * jnp.sin(S5_L * (li * dt))
        a16_ref[0, 2 * d:2 * d + 1, :] = a_re
        a16_ref[0, 2 * d + 1:2 * d + 2, :] = lane_sign * a_im
        lrc = lam_col_ref[d, 0]
        lic = lam_col_ref[2 + d, 0]

        def c_pow(tau_c):
            qm = jnp.exp(tau_c * (lrc * dt))
            q_re, q_im = qm * jnp.cos(tau_c * (lic * dt)), qm * jnp.sin(tau_c * (lic * dt))
            return row_sign * c1 * q_re - c2 * q_im

        wc_ref[0, d * 2 * S5_P:(d + 1) * 2 * S5_P, :] = c_pow(t_of_col + 1.0 if d == 0 else S5_L - t_of_col)
        m_all = jnp.dot(bk[0:S5_H], c_pow(t_of_col if d == 0 else (S5_L - 1.0) - t_of_col),
                        precision=HIGHEST, preferred_element_type=F32)
        lane = lax.broadcasted_iota(jnp.int32, (S5_H, lh), 1)
        blocks = []
        for s in range(S5_L):
            if d == 0:
                shift, keep = S5_H * s, lane >= S5_H * s
            else:
                shift, keep = (S5_H * (s + 1)) % lh, lane < S5_H * (s + 1)
            rolled = m_all if shift == 0 else pltpu.roll(m_all, shift, axis=1)
            blocks.append(jnp.where(keep, rolled, 0.0))
        td = jnp.concatenate(blocks, axis=0)
        toep = td if toep is None else toep + td
    diag = lax.broadcasted_iota(jnp.int32, (lh, lh), 0) == lax.broadcasted_iota(jnp.int32, (lh, lh), 1)
    wc_ref[0, 4 * S5_P:, :] = toep + jnp.where(diag, dt_ref[0], 0.0)


def _s5_prep(lam_re, lam_im, log_dt, b_re, b_im, c_re, c_im, d_skip):
    lh = S5_L * S5_H
    dup = lambda z: jnp.concatenate([z, z], axis=-1)
    lam = jnp.concatenate([lam_re, lam_im], axis=0)
    lam_row = dup(lam)[:, :, None, :]
    lam_col = dup(lam)[:, :, :, None]
    bt_re, bt_im = jnp.swapaxes(b_re, 1, 2), jnp.swapaxes(b_im, 1, 2)
    b1 = jnp.tile(jnp.concatenate([bt_re, bt_im], axis=-1), (1, S5_L, 1))
    b2 = jnp.tile(jnp.concatenate([bt_im, bt_re], axis=-1), (1, S5_L, 1))
    ct_re, ct_im = jnp.swapaxes(c_re, 1, 2), jnp.swapaxes(c_im, 1, 2)
    c1 = jnp.tile(jnp.concatenate([ct_re, ct_im], axis=1), (1, 1, S5_L))
    c2 = jnp.tile(jnp.concatenate([ct_im, ct_re], axis=1), (1, 1, S5_L))
    d_tiled = jnp.tile(d_skip.reshape(S5_G, 1, S5_H), (1, 1, S5_L))
    grp = lambda shape: pl.BlockSpec((1,) + shape, lambda g: (g, 0, 0))
    return pl.pallas_call(
        _s5_prep_kernel,
        grid=(S5_G,),
        in_specs=[pl.BlockSpec(memory_space=pltpu.SMEM),
                  pl.BlockSpec((4, 1, 1, 2 * S5_P), lambda g: (0, g, 0, 0)),
                  pl.BlockSpec((4, 1, 2 * S5_P, 1), lambda g: (0, g, 0, 0)),
                  grp((lh, 2 * S5_P)), grp((lh, 2 * S5_P)),
                  grp((2 * S5_P, lh)), grp((2 * S5_P, lh)),
                  grp((1, lh))],
        out_specs=[grp((lh, 4 * S5_P)), grp((4 * S5_P + lh, lh)), grp((4, 2 * S5_P))],
        out_shape=[jax.ShapeDtypeStruct((S5_G, lh, 4 * S5_P), F32),
                   jax.ShapeDtypeStruct((S5_G, 4 * S5_P + lh, lh), F32),
                   jax.ShapeDtypeStruct((S5_G, 4, 2 * S5_P), F32)],
        compiler_params=_params(("parallel",), 32),
        name="s5_prep",
    )(log_dt, lam_row, lam_col, b1, b2, c1, c2, d_tiled)


def _s5_local_kernel(u_ref, wst_ref, xf_ref, xb_ref):
    r = jnp.dot(u_ref[0], wst_ref[0], precision=HIGHEST, preferred_element_type=F32)
    xf_ref[...] = r[:, :2 * S5_P]
    xb_ref[...] = r[:, 2 * S5_P:]


def _s5_local(ug, wst):
    rows = ug.shape[1]
    lh = S5_L * S5_H
    col = pl.BlockSpec((rows, 2 * S5_P), lambda g: (0, g))
    shape = jax.ShapeDtypeStruct((rows, S5_G * 2 * S5_P), F32)
    return pl.pallas_call(
        _s5_local_kernel,
        grid=(S5_G,),
        in_specs=[pl.BlockSpec((1, rows, lh), lambda g: (g, 0, 0)),
                  pl.BlockSpec((1, lh, 4 * S5_P), lambda g: (g, 0, 0))],
        out_specs=[col, col],
        out_shape=[shape, shape],
        compiler_params=_params(("parallel",), 32),
        name="s5_local_state",
    )(ug, wst)


S5_SCAN_BLOCK = 8


def _s5_scan_kernel(xf_ref, xb_ref, a_ref, pf_ref, pb_ref, sf_ref, sb_ref):
    @pl.when(pl.program_id(0) == 0)
    def _():
        sf_ref[...] = jnp.zeros_like(sf_ref)
        sb_ref[...] = jnp.zeros_like(sb_ref)

    def step(s, x, aa, ab):
        return aa * s + ab * pltpu.roll(s, S5_P, axis=1) + x

    s = sf_ref[...]
    for j in range(S5_SCAN_BLOCK):
        pf_ref[j] = s
        s = step(s, xf_ref[j], a_ref[0], a_ref[1])
    sf_ref[...] = s
    s = sb_ref[...]
    for j in reversed(range(S5_SCAN_BLOCK)):
        pb_ref[j] = s
        s = step(s, xb_ref[j], a_ref[2], a_ref[3])
    sb_ref[...] = s


def _s5_scan(xf, xb, a16):
    n_chunks, rows, lanes = xf.shape
    nblk = n_chunks // S5_SCAN_BLOCK
    ctx_blk = (CTX_LEN // S5_L) // S5_SCAN_BLOCK
    fwd = lambda i: (i, 0, 0)
    bwd = lambda i: (jnp.where(i < ctx_blk, ctx_blk - 1 - i, nblk - 1 + ctx_blk - i), 0, 0)
    blk = (S5_SCAN_BLOCK, rows, lanes)
    shape = jax.ShapeDtypeStruct(xf.shape, F32)
    return pl.pallas_call(
        _s5_scan_kernel,
        grid=(nblk,),
        in_specs=[pl.BlockSpec(blk, fwd), pl.BlockSpec(blk, bwd),
                  pl.BlockSpec((4, rows, lanes), lambda i: (0, 0, 0))],
        out_specs=[pl.BlockSpec(blk, fwd), pl.BlockSpec(blk, bwd)],
        out_shape=[shape, shape],
        scratch_shapes=[pltpu.VMEM((rows, lanes), F32), pltpu.VMEM((rows, lanes), F32)],
        compiler_params=_params(("arbitrary",), 32),
        name="s5_chunk_scan",
    )(xf, xb, a16)


def _s5_out_kernel(pf_ref, pb_ref, u_ref, wc_ref, y_ref):
    dot = functools.partial(jnp.dot, precision=HIGHEST, preferred_element_type=F32)
    y = (dot(pf_ref[...], wc_ref[0, 0:2 * S5_P, :])
         + dot(pb_ref[...], wc_ref[0, 2 * S5_P:4 * S5_P, :])
         + dot(u_ref[0], wc_ref[0, 4 * S5_P:, :]))
    y_ref[0] = jax.nn.gelu(y)


def _s5_out(pf, pb, ug, wc):
    rows = ug.shape[1]
    lh = S5_L * S5_H
    col = pl.BlockSpec((rows, 2 * S5_P), lambda g: (0, g))
    return pl.pallas_call(
        _s5_out_kernel,
        grid=(S5_G,),
        in_specs=[col, col,
                  pl.BlockSpec((1, rows, lh), lambda g: (g, 0, 0)),
                  pl.BlockSpec((1, 4 * S5_P + lh, lh), lambda g: (g, 0, 0))],
        out_specs=pl.BlockSpec((1, rows, lh), lambda g: (g, 0, 0)),
        out_shape=jax.ShapeDtypeStruct((S5_G, rows, lh), F32),
        compiler_params=_params(("parallel",), 32),
        name="s5_output",
    )(pf, pb, ug, wc)


def _s5_glu_kernel(y_ref, w_ref, o_ref):
    y = y_ref[...]
    gate = jnp.dot(y.astype(BF16), w_ref[...], preferred_element_type=F32)
    o_ref[...] = (y * jax.nn.sigmoid(gate)).astype(o_ref.dtype)


def _s5_glu(y, w_glu, tm=1024):
    m = y.shape[0]
    return pl.pallas_call(
        _s5_glu_kernel,
        grid=(m // tm,),
        in_specs=[pl.BlockSpec((tm, D_A), lambda i: (i, 0)),
                  pl.BlockSpec((D_A, D_A), lambda i: (0, 0))],
        out_specs=pl.BlockSpec((tm, D_A), lambda i: (i, 0)),
        out_shape=jax.ShapeDtypeStruct((m, D_A), BF16),
        compiler_params=_params(("parallel",), 32),
        name="s5_glu",
    )(y, w_glu)


def _s5_branch(z, lam_re, lam_im, log_dt, b_re, b_im, c_re, c_im, d_skip, w_glu_bf16):
    wst, wc, a16 = _s5_prep(lam_re, lam_im, log_dt, b_re, b_im, c_re, c_im, d_skip)
    nc_ctx, nc_lat = CTX_LEN // S5_L, SEQ // S5_L
    n_chunks = nc_ctx + nc_lat
    u = z[:, :D_A]
    u_ctx = u[:N_CTX_ROWS].reshape(BATCH, nc_ctx, S5_L, S5_G, S5_H).transpose(3, 1, 0, 2, 4)
    u_lat = u[N_CTX_ROWS:].reshape(BATCH, nc_lat, S5_L, S5_G, S5_H).transpose(3, 1, 0, 2, 4)
    ug = jnp.concatenate([u_ctx, u_lat], axis=1).reshape(S5_G, n_chunks * BATCH, S5_L * S5_H)
    xf, xb = _s5_local(ug, wst)
    as_scan = lambda x: x.reshape(n_chunks, BATCH * S5_G, 2 * S5_P)
    a16_rows = jnp.tile(jnp.swapaxes(a16, 0, 1), (1, BATCH, 1))
    pf, pb = _s5_scan(as_scan(xf), as_scan(xb), a16_rows)
    as_cols = lambda x: x.reshape(n_chunks * BATCH, S5_G * 2 * S5_P)
    yg = _s5_out(as_cols(pf), as_cols(pb), ug, wc)
    y_lat = yg.reshape(S5_G, n_chunks, BATCH, S5_L, S5_H)[:, nc_ctx:]
    y_lat = y_lat.transpose(2, 1, 3, 0, 4).reshape(N_LAT_ROWS, D_A)
    return _s5_glu(y_lat, w_glu_bf16)


def _hgrn_kernel(q_ref, v_ref, ff_ref, fb_ref, og_ref, lg_ref, gn_ref, o_ref, acc_ref, *, layer):
    c = HG_CHUNK
    n_chunks = T_ALL // c
    n_ctx_chunks = CTX_LEN // c
    row = lax.broadcasted_iota(jnp.int32, (c, c), 0)
    col = lax.broadcasted_iota(jnp.int32, (c, c), 1)
    causal = row >= col
    anti = row <= col
    tri = (causal.astype(F32), anti.astype(F32))
    nt = (((1,), (1,)), ((), ()))
    tn = (((0,), (0,)), ((), ()))

    def lower_bound(d):
        logits = [lg_ref[d, j, 0] for j in range(lg_ref.shape[1])]
        top = functools.reduce(jnp.maximum, logits)
        e = [jnp.exp(l - top) for l in logits]
        return sum(e[:layer + 1]) / sum(e)

    lb = (lower_bound(0), lower_bound(1))

    def chunk(c0, fpre_ref, d, state_t):
        q = q_ref[0, 0, pl.ds(c0, c), :]
        v = v_ref[0, 0, pl.ds(c0, c), :].astype(BF16)
        lbv = lb[d]
        f = lbv + (1.0 - lbv) * jax.nn.sigmoid(fpre_ref[0, 0, pl.ds(c0, c), :])
        k = 1.0 - f
        b = jnp.dot(tri[d], jnp.log(f), precision=HIGHEST, preferred_element_type=F32)
        mid = c // 2 - 1 if d == 0 else c // 2
        last = c - 1 if d == 0 else 0
        b_mid, b_last = b[mid:mid + 1], b[last:last + 1]
        qe = (q * jnp.exp(b - b_mid)).astype(BF16)
        ke = (k * jnp.exp(b_mid - b)).astype(BF16)
        scores = lax.dot_general(qe, ke, nt, preferred_element_type=F32)
        scores = jnp.where(causal if d == 0 else anti, scores, 0.0)
        o = jnp.dot(scores.astype(BF16), v, preferred_element_type=F32)
        qb = (q * jnp.exp(b)).astype(BF16)
        o = o + lax.dot_general(qb, state_t.astype(BF16), nt, preferred_element_type=F32)
        kd = (k * jnp.exp(b_last - b)).astype(BF16)
        kv_t = lax.dot_general(v, kd, tn, preferred_element_type=F32)
        acc_ref[pl.ds(c0, c), :] += o
        return state_t * jnp.exp(b_last) + kv_t

    acc_ref[...] = jnp.zeros_like(acc_ref)

    def body(i, carry):
        sf, sb = carry
        sf = chunk(pl.multiple_of(i * c, c), ff_ref, 0, sf)
        jb = jnp.where(i < n_ctx_chunks, n_ctx_chunks - 1 - i, n_chunks - 1 + n_ctx_chunks - i)
        sb = chunk(pl.multiple_of(jb * c, c), fb_ref, 1, sb)
        return sf, sb

    zero = jnp.zeros((HG_DK, HG_DK), F32)
    lax.fori_loop(0, n_chunks, body, (zero, zero))

    o = acc_ref[CTX_LEN:, :]
    o = o * lax.rsqrt(jnp.mean(o * o, axis=-1, keepdims=True) + EPS)
    o_ref[0, 0] = (o * gn_ref[0] * jax.nn.silu(og_ref[0, 0])).astype(o_ref.dtype)


def _hgrn(q, v, ff, fb, og, lb_logits, gn, layer):
    seq_spec = pl.BlockSpec((1, 1, T_ALL, HG_DK), lambda b, h: (b, h, 0, 0))
    lat_spec = pl.BlockSpec((1, 1, SEQ, HG_DK), lambda b, h: (b, h, 0, 0))
    n_slots = lb_logits.shape[1]
    return pl.pallas_call(
        functools.partial(_hgrn_kernel, layer=layer),
        grid=(BATCH, HG_HEADS),
        in_specs=[seq_spec, seq_spec, seq_spec, seq_spec, lat_spec,
                  pl.BlockSpec((2, n_slots, 1, 1, HG_DK), lambda b, h: (0, 0, h, 0, 0)),
                  pl.BlockSpec((1, 1, HG_DK), lambda b, h: (h, 0, 0))],
        out_specs=lat_spec,
        out_shape=jax.ShapeDtypeStruct((BATCH, HG_HEADS, SEQ, HG_DK), BF16),
        scratch_shapes=[pltpu.VMEM((T_ALL, HG_DK), F32)],
        compiler_params=_params(("parallel", "parallel"), 40),
        name="hgrn2",
    )(q, v, ff, fb, og, lb_logits, gn)


def _hgrn_branch(z, lb_logits, hg_norm, layer):
    def heads_ctx(a):
        return a[:N_CTX_ROWS].reshape(BATCH, CTX_LEN, HG_HEADS, HG_DK).transpose(0, 2, 1, 3)

    def heads_lat_colmajor(a):
        a = a[N_CTX_ROWS:].reshape(BATCH, GRID_ROWS, GRID_W, HG_HEADS, HG_DK)
        return a.transpose(0, 3, 2, 1, 4).reshape(BATCH, HG_HEADS, SEQ, HG_DK)

    def scan_order(col0):
        a = z[:, col0:col0 + D_B]
        return jnp.concatenate([heads_ctx(a), heads_lat_colmajor(a)], axis=2)

    q, v, ff, fb = (scan_order(D_A + i * D_B) for i in range(4))
    og = heads_lat_colmajor(z[:, D_A + 4 * D_B:D_A + 5 * D_B])
    lg = lb_logits.astype(F32).reshape(2, lb_logits.shape[1], HG_HEADS, 1, HG_DK)
    y = _hgrn(q, v, ff, fb, og, lg, hg_norm.reshape(HG_HEADS, 1, HG_DK), layer)
    y = y.reshape(BATCH, HG_HEADS, GRID_W, GRID_ROWS, HG_DK).transpose(0, 3, 2, 1, 4)
    return y.reshape(N_LAT_ROWS, D_B)


def _pad_ff_cols(w):
    return jnp.pad(w.astype(BF16), ((0, 0), (0, D_FF_PAD - D_FF)))


def _pad_ff_rows(w):
    return jnp.pad(w.astype(BF16), ((0, D_FF_PAD - D_FF), (0, 0)))


def kernel(x, c, ctx, c_ctx, w_ada, b_ada, norm_ffn1, w1_ffn1, w3_ffn1, w2_ffn1, norm_mix, w_in,
           s5_lam_re, s5_lam_im, s5_log_dt, s5_b_re, s5_b_im, s5_c_re, s5_c_im, s5_d, s5_w_glu,
           hg_lb_logits, hg_norm, w_proj_a, w_proj_b, w_out, norm_ffn2, w1_ffn2, w3_ffn2, w2_ffn2,
           norm_final):
    layer = 0
    c8 = jnp.concatenate([c, c_ctx[None], jnp.zeros((8 - BATCH - 1, D_MODEL), F32)], axis=0)
    mod = _ada(c8, w_ada[layer], b_ada[layer]).reshape(8, N_MOD, D_MODEL)
    mod = jnp.concatenate([mod[BATCH:BATCH + 1], mod[:BATCH]], axis=0)
    m = [mod[:, i].reshape(BATCH + 1, 1, D_MODEL) for i in range(N_MOD)]

    h0 = jnp.concatenate([ctx.reshape(N_CTX_ROWS, D_MODEL), x.reshape(N_LAT_ROWS, D_MODEL)], axis=0)
    tile = 1024
    lat_off = N_CTX_ROWS // tile

    z1 = _norm_mod(h0, norm_ffn1[layer], m[0], m[1], _group_full(512))
    g1 = _swiglu_up(z1, _pad_ff_cols(w1_ffn1[layer]), _pad_ff_cols(w3_ffn1[layer]))
    h1 = _mm_res(g1, _pad_ff_rows(w2_ffn1[layer]), h0, 0, m[2], 0.5, _group_full(tile),
                 tile, 1024, D_FF_PAD // 4, 52)

    z2 = _norm_mod(h1, norm_mix[layer], m[3], m[4], _group_full(512))
    z = _mm(z2, w_in[layer].astype(BF16), F32, tile, 1024, 52, "mixer_in_proj")

    y_a = _s5_branch(z, s5_lam_re[layer], s5_lam_im[layer], s5_log_dt[layer], s5_b_re[layer],
                     s5_b_im[layer], s5_c_re[layer], s5_c_im[layer], s5_d[layer],
                     s5_w_glu[layer].astype(BF16))
    y_b = _hgrn_branch(z, hg_lb_logits, hg_norm[layer], layer)

    ga_tile = (D_A + 5 * D_B) // 512
    merged = _merge(y_a, w_proj_a[layer].astype(BF16), y_b, w_proj_b[layer].astype(BF16), z,
                    ga_tile, ga_tile + D_MODEL // 512, lat_off)
    h2 = _mm_res(merged, w_out[layer].astype(BF16), h1, lat_off, m[5], 1.0, _group_lat(tile),
                 tile, 512, D_MODEL, 48)

    z3 = _norm_mod(h2, norm_ffn2[layer], m[6], m[7], _group_lat(512))
    g2 = _swiglu_up(z3, _pad_ff_cols(w1_ffn2[layer]), _pad_ff_cols(w3_ffn2[layer]))
    h3 = _mm_res(g2, _pad_ff_rows(w2_ffn2[layer]), h2, 0, m[8], 0.5, _group_lat(tile),
                 tile, 1024, D_FF_PAD // 4, 52)
    return _final_norm(h3, norm_final).reshape(BATCH, SEQ, D_MODEL)
```

```python
import functools
import math

import jax
import jax.numpy as jnp
from jax import lax
from jax.experimental import pallas as pl
from jax.experimental.pallas import tpu as pltpu

F32 = jnp.float32
BF16 = jnp.bfloat16
HIGHEST = lax.Precision.HIGHEST

D_MODEL = 4096
BATCH = 4
SEQ = 2048
GRID_W = 64
GRID_ROWS = SEQ // GRID_W
CTX_LEN = 256
T_ALL = CTX_LEN + SEQ
D_FF = 11008
D_FF_PAD = 11264
D_A = 1024
S5_H = 16
S5_G = D_A // S5_H
S5_P = 64
S5_L = 16
D_B = 2048
HG_DK = 128
HG_HEADS = D_B // HG_DK
HG_CHUNK = 64
N_MOD = 9
N_IN = D_A + 5 * D_B + 2 * D_MODEL
EPS = 1e-6

N_CTX_ROWS = BATCH * CTX_LEN
N_LAT_ROWS = BATCH * SEQ
N_ROWS = N_CTX_ROWS + N_LAT_ROWS
MIB = 1024 * 1024


def _params(semantics, vmem_mib):
    return pltpu.CompilerParams(dimension_semantics=semantics, vmem_limit_bytes=vmem_mib * MIB)


def _group_full(tm):
    n_lat_tiles, per_batch = N_LAT_ROWS // tm, SEQ // tm
    return lambda i: jnp.where(i < n_lat_tiles, 1 + i // per_batch, 0)


def _group_lat(tm):
    per_batch = SEQ // tm
    return lambda i: 1 + i // per_batch


def _ada_kernel(c_ref, w_ref, b_ref, o_ref):
    sc = jax.nn.silu(c_ref[...]).astype(BF16)
    o_ref[...] = jnp.dot(sc, w_ref[...].astype(BF16), preferred_element_type=F32) + b_ref[...]


def _ada(c8, w_ada, b_ada):
    n = w_ada.shape[1]
    tn = 512
    return pl.pallas_call(
        _ada_kernel,
        grid=(n // tn,),
        in_specs=[pl.BlockSpec((8, D_MODEL), lambda j: (0, 0)),
                  pl.BlockSpec((D_MODEL, tn), lambda j: (0, j)),
                  pl.BlockSpec((1, tn), lambda j: (0, j))],
        out_specs=pl.BlockSpec((8, tn), lambda j: (0, j)),
        out_shape=jax.ShapeDtypeStruct((8, n), F32),
        compiler_params=_params(("parallel",), 40),
        name="ada_table",
    )(c8, w_ada, b_ada.reshape(1, n))


def _norm_mod_kernel(x_ref, g_ref, sh_ref, sc_ref, o_ref):
    x = x_ref[...]
    var = jnp.mean(x * x, axis=-1, keepdims=True)
    y = x * lax.rsqrt(var + EPS) * g_ref[...]
    o_ref[...] = (y * (1.0 + sc_ref[0]) + sh_ref[0]).astype(o_ref.dtype)


def _norm_mod(x, gain, shift, scale, group_fn, tm=512):
    rows = x.shape[0]
    vec = pl.BlockSpec((1, 1, D_MODEL), lambda i: (group_fn(i), 0, 0))
    return pl.pallas_call(
        _norm_mod_kernel,
        grid=(rows // tm,),
        in_specs=[pl.BlockSpec((tm, D_MODEL), lambda i: (i, 0)),
                  pl.BlockSpec((1, D_MODEL), lambda i: (0, 0)),
                  vec, vec],
        out_specs=pl.BlockSpec((tm, D_MODEL), lambda i: (i, 0)),
        out_shape=jax.ShapeDtypeStruct((rows, D_MODEL), BF16),
        compiler_params=_params(("parallel",), 40),
        name="norm_modulate",
    )(x, gain.reshape(1, D_MODEL), shift, scale)


def _final_norm_kernel(x_ref, g_ref, o_ref):
    x = x_ref[...]
    var = jnp.mean(x * x, axis=-1, keepdims=True)
    o_ref[...] = x * lax.rsqrt(var + EPS) * g_ref[...]


def _final_norm(x, gain, tm=512):
    rows = x.shape[0]
    return pl.pallas_call(
        _final_norm_kernel,
        grid=(rows // tm,),
        in_specs=[pl.BlockSpec((tm, D_MODEL), lambda i: (i, 0)),
                  pl.BlockSpec((1, D_MODEL), lambda i: (0, 0))],
        out_specs=pl.BlockSpec((tm, D_MODEL), lambda i: (i, 0)),
        out_shape=jax.ShapeDtypeStruct((rows, D_MODEL), F32),
        compiler_params=_params(("parallel",), 40),
        name="final_norm",
    )(x, gain.reshape(1, D_MODEL))


def _swiglu_up_kernel(z_ref, w1_ref, w3_ref, o_ref):
    z = z_ref[...]
    h1 = jnp.dot(z, w1_ref[...], preferred_element_type=F32)
    h3 = jnp.dot(z, w3_ref[...], preferred_element_type=F32)
    o_ref[...] = (jax.nn.silu(h1) * h3).astype(o_ref.dtype)


def _swiglu_up(z, w1, w3, tm=1024, tn=512):
    m, k = z.shape
    n = w1.shape[1]
    return pl.pallas_call(
        _swiglu_up_kernel,
        grid=(m // tm, n // tn),
        in_specs=[pl.BlockSpec((tm, k), lambda i, j: (i, 0)),
                  pl.BlockSpec((k, tn), lambda i, j: (0, j)),
                  pl.BlockSpec((k, tn), lambda i, j: (0, j))],
        out_specs=pl.BlockSpec((tm, tn), lambda i, j: (i, j)),
        out_shape=jax.ShapeDtypeStruct((m, n), BF16),
        compiler_params=_params(("parallel", "arbitrary"), 52),
        name="swiglu_up",
    )(z, w1, w3)


def _mm_res_kernel(a_ref, w_ref, res_ref, gate_ref, o_ref, acc_ref, *, scale, nk):
    k = pl.program_id(2)

    @pl.when(k == 0)
    def _():
        acc_ref[...] = jnp.zeros_like(acc_ref)

    acc_ref[...] += jnp.dot(a_ref[...], w_ref[...], preferred_element_type=F32)

    @pl.when(k == nk - 1)
    def _():
        o_ref[...] = res_ref[...] + (scale * gate_ref[0]) * acc_ref[...]


def _mm_res(a, w, res, res_tile_off, gate, scale, group_fn, tm, tn, tk, vmem_mib):
    m, k = a.shape
    n = w.shape[1]
    nk = k // tk
    return pl.pallas_call(
        functools.partial(_mm_res_kernel, scale=scale, nk=nk),
        grid=(m // tm, n // tn, nk),
        in_specs=[pl.BlockSpec((tm, tk), lambda i, j, kk: (i, kk)),
                  pl.BlockSpec((tk, tn), lambda i, j, kk: (kk, j)),
                  pl.BlockSpec((tm, tn), lambda i, j, kk: (i + res_tile_off, j)),
                  pl.BlockSpec((1, 1, tn), lambda i, j, kk: (group_fn(i), 0, j))],
        out_specs=pl.BlockSpec((tm, tn), lambda i, j, kk: (i, j)),
        out_shape=jax.ShapeDtypeStruct((m, n), F32),
        scratch_shapes=[pltpu.VMEM((tm, tn), F32)],
        compiler_params=_params(("parallel", "parallel", "arbitrary"), vmem_mib),
        name="matmul_gated_residual",
    )(a, w, res, gate)


def _mm_kernel(a_ref, w_ref, o_ref):
    o_ref[...] = jnp.dot(a_ref[...], w_ref[...], preferred_element_type=F32).astype(o_ref.dtype)


def _mm(a, w, out_dtype, tm, tn, vmem_mib, name):
    m, k = a.shape
    n = w.shape[1]
    return pl.pallas_call(
        _mm_kernel,
        grid=(m // tm, n // tn),
        in_specs=[pl.BlockSpec((tm, k), lambda i, j: (i, 0)),
                  pl.BlockSpec((k, tn), lambda i, j: (0, j))],
        out_specs=pl.BlockSpec((tm, tn), lambda i, j: (i, j)),
        out_shape=jax.ShapeDtypeStruct((m, n), out_dtype),
        compiler_params=_params(("parallel", "arbitrary"), vmem_mib),
        name=name,
    )(a, w)


def _merge_kernel(ya_ref, wa_ref, yb_ref, wb_ref, ga_ref, gb_ref, o_ref):
    pa = jnp.dot(ya_ref[...], wa_ref[...], preferred_element_type=F32)
    pb = jnp.dot(yb_ref[...], wb_ref[...], preferred_element_type=F32)
    o_ref[...] = (jax.nn.sigmoid(ga_ref[...]) * pa + jax.nn.sigmoid(gb_ref[...]) * pb).astype(o_ref.dtype)


def _merge(ya, wa, yb, wb, z, ga_col_tile, gb_col_tile, z_row_tile_off, tm=1024, tn=512):
    m = ya.shape[0]
    n = wa.shape[1]
    return pl.pallas_call(
        _merge_kernel,
        grid=(m // tm, n // tn),
        in_specs=[pl.BlockSpec((tm, D_A), lambda i, j: (i, 0)),
                  pl.BlockSpec((D_A, tn), lambda i, j: (0, j)),
                  pl.BlockSpec((tm, D_B), lambda i, j: (i, 0)),
                  pl.BlockSpec((D_B, tn), lambda i, j: (0, j)),
                  pl.BlockSpec((tm, tn), lambda i, j: (i + z_row_tile_off, ga_col_tile + j)),
                  pl.BlockSpec((tm, tn), lambda i, j: (i + z_row_tile_off, gb_col_tile + j))],
        out_specs=pl.BlockSpec((tm, tn), lambda i, j: (i, j)),
        out_shape=jax.ShapeDtypeStruct((m, n), BF16),
        compiler_params=_params(("parallel", "arbitrary"), 48),
        name="gated_merge",
    )(ya, wa, yb, wb, z, z)


def _s5_prep_kernel(ldt_ref, lam_row_ref, lam_col_ref, b1_ref, b2_ref, c1_ref, c2_ref, dt_ref,
                    wst_ref, wc_ref, a16_ref):
    g = pl.program_id(0)
    lh = S5_L * S5_H
    lane_sign = jnp.where(lax.broadcasted_iota(jnp.int32, (1, 2 * S5_P), 1) < S5_P, -1.0, 1.0)
    row_sign = jnp.where(lax.broadcasted_iota(jnp.int32, (2 * S5_P, 1), 0) < S5_P, 1.0, -1.0)
    log2_h = S5_H.bit_length() - 1
    s_of_row = lax.shift_right_logical(lax.broadcasted_iota(jnp.int32, (lh, 1), 0), log2_h).astype(F32)
    t_of_col = lax.shift_right_logical(lax.broadcasted_iota(jnp.int32, (1, lh), 1), log2_h).astype(F32)
    b1 = b1_ref[0]
    b2 = b2_ref[0]
    c1 = c1_ref[0]
    c2 = c2_ref[0]
    toep = None
    for d in range(2):
        dt = jnp.exp(jnp.full((1, 1), ldt_ref[d, g], F32))
        lr = lam_row_ref[d, 0]
        li = lam_row_ref[2 + d, 0]
        mag = jnp.exp(lr * dt)
        ab_re, ab_im = mag * jnp.cos(li * dt), mag * jnp.sin(li * dt)
        den = lr * lr + li * li
        nr = ab_re - 1.0
        kr = (nr * lr + ab_im * li) / den
        ki = (ab_im * lr - nr * li) / den
        bk = kr * b1 + lane_sign * ki * b2
        bk_sw = lane_sign * (kr * b2 - lane_sign * ki * b1)
        tau = (S5_L - 1.0) - s_of_row if d == 0 else s_of_row
        pm = jnp.exp(tau * (lr * dt))
        p_re, p_im = pm * jnp.cos(tau * (li * dt)), pm * jnp.sin(tau * (li * dt))
        wst_ref[0, :, d * 2 * S5_P:(d + 1) * 2 * S5_P] = p_re * bk + p_im * bk_sw
        m16 = jnp.exp(S5_L * (lr * dt))
        a_re, a_im = m16 * jnp.cos(S5_L * (li * dt)), m16 * jnp.sin(S5_L * (li * dt))
        a16_ref[0, 2 * d:2 * d + 1, :] = a_re
        a16_ref[0, 2 * d + 1:2 * d + 2, :] = lane_sign * a_im
        lrc = lam_col_ref[d, 0]
        lic = lam_col_ref[2 + d, 0]

        def c_pow(tau_c):
            qm = jnp.exp(tau_c * (lrc * dt))
            q_re, q_im = qm * jnp.cos(tau_c * (lic * dt)), qm * jnp.sin(tau_c * (lic * dt))
            return row_sign * c1 * q_re - c2 * q_im

        wc_ref[0, d * 2 * S5_P:(d + 1) * 2 * S5_P, :] = c_pow(t_of_col + 1.0 if d == 0 else S5_L - t_of_col)
        m_all = jnp.dot(bk[0:S5_H], c_pow(t_of_col if d == 0 else (S5_L - 1.0) - t_of_col),
                        precision=HIGHEST, preferred_element_type=F32)
        lane = lax.broadcasted_iota(jnp.int32, (S5_H, lh), 1)
        blocks = []
        for s in range(S5_L):
            if d == 0:
                shift, keep = S5_H * s, lane >= S5_H * s
            else:
                shift, keep = (S5_H * (s + 1)) % lh, lane < S5_H * (s + 1)
            rolled = m_all if shift == 0 else pltpu.roll(m_all, shift, axis=1)
            blocks.append(jnp.where(keep, rolled, 0.0))
        td = jnp.concatenate(blocks, axis=0)
        toep = td if toep is None else toep + td
    diag = lax.broadcasted_iota(jnp.int32, (lh, lh), 0) == lax.broadcasted_iota(jnp.int32, (lh, lh), 1)
    wc_ref[0, 4 * S5_P:, :] = toep + jnp.where(diag, dt_ref[0], 0.0)


def _s5_prep(lam_re, lam_im, log_dt, b_re, b_im, c_re, c_im, d_skip):
    lh = S5_L * S5_H
    dup = lambda z: jnp.concatenate([z, z], axis=-1)
    lam = jnp.concatenate([lam_re, lam_im], axis=0)
    lam_row = dup(lam)[:, :, None, :]
    lam_col = dup(lam)[:, :, :, None]
    bt_re, bt_im = jnp.swapaxes(b_re, 1, 2), jnp.swapaxes(b_im, 1, 2)
    b1 = jnp.tile(jnp.concatenate([bt_re, bt_im], axis=-1), (1, S5_L, 1))
    b2 = jnp.tile(jnp.concatenate([bt_im, bt_re], axis=-1), (1, S5_L, 1))
    ct_re, ct_im = jnp.swapaxes(c_re, 1, 2), jnp.swapaxes(c_im, 1, 2)
    c1 = jnp.tile(jnp.concatenate([ct_re, ct_im], axis=1), (1, 1, S5_L))
    c2 = jnp.tile(jnp.concatenate([ct_im, ct_re], axis=1), (1, 1, S5_L))
    d_tiled = jnp.tile(d_skip.reshape(S5_G, 1, S5_H), (1, 1, S5_L))
    grp = lambda shape: pl.BlockSpec((1,) + shape, lambda g: (g, 0, 0))
    return pl.pallas_call(
        _s5_prep_kernel,
        grid=(S5_G,),
        in_specs=[pl.BlockSpec(memory_space=pltpu.SMEM),
                  pl.BlockSpec((4, 1, 1, 2 * S5_P), lambda g: (0, g, 0, 0)),
                  pl.BlockSpec((4, 1, 2 * S5_P, 1), lambda g: (0, g, 0, 0)),
                  grp((lh, 2 * S5_P)), grp((lh, 2 * S5_P)),
                  grp((2 * S5_P, lh)), grp((2 * S5_P, lh)),
                  grp((1, lh))],
        out_specs=[grp((lh, 4 * S5_P)), grp((4 * S5_P + lh, lh)), grp((4, 2 * S5_P))],
        out_shape=[jax.ShapeDtypeStruct((S5_G, lh, 4 * S5_P), F32),
                   jax.ShapeDtypeStruct((S5_G, 4 * S5_P + lh, lh), F32),
                   jax.ShapeDtypeStruct((S5_G, 4, 2 * S5_P), F32)],
        compiler_params=_params(("parallel",), 32),
        name="s5_prep",
    )(log_dt, lam_row, lam_col, b1, b2, c1, c2, d_tiled)


def _s5_local_kernel(u_ref, wst_ref, xf_ref, xb_ref):
    r = jnp.dot(u_ref[0], wst_ref[0], precision=HIGHEST, preferred_element_type=F32)
    xf_ref[...] = r[:, :2 * S5_P]
    xb_ref[...] = r[:, 2 * S5_P:]


def _s5_local(ug, wst):
    rows = ug.shape[1]
    lh = S5_L * S5_H
    col = pl.BlockSpec((rows, 2 * S5_P), lambda g: (0, g))
    shape = jax.ShapeDtypeStruct((rows, S5_G * 2 * S5_P), F32)
    return pl.pallas_call(
        _s5_local_kernel,
        grid=(S5_G,),
        in_specs=[pl.BlockSpec((1, rows, lh), lambda g: (g, 0, 0)),
                  pl.BlockSpec((1, lh, 4 * S5_P), lambda g: (g, 0, 0))],
        out_specs=[col, col],
        out_shape=[shape, shape],
        compiler_params=_params(("parallel",), 32),
        name="s5_local_state",
    )(ug, wst)


S5_SCAN_BLOCK = 8


def _s5_scan_kernel(xf_ref, xb_ref, a_ref, pf_ref, pb_ref, sf_ref, sb_ref):
    @pl.when(pl.program_id(0) == 0)
    def _():
        sf_ref[...] = jnp.zeros_like(sf_ref)
        sb_ref[...] = jnp.zeros_like(sb_ref)

    def step(s, x, aa, ab):
        return aa * s + ab * pltpu.roll(s, S5_P, axis=1) + x

    s = sf_ref[...]
    for j in range(S5_SCAN_BLOCK):
        pf_ref[j] = s
        s = step(s, xf_ref[j], a_ref[0], a_ref[1])
    sf_ref[...] = s
    s = sb_ref[...]
    for j in reversed(range(S5_SCAN_BLOCK)):
        pb_ref[j] = s
        s = step(s, xb_ref[j], a_ref[2], a_ref[3])
    sb_ref[...] = s


def _s5_scan(xf, xb, a16):
    n_chunks, rows, lanes = xf.shape
    nblk = n_chunks // S5_SCAN_BLOCK
    ctx_blk = (CTX_LEN // S5_L) // S5_SCAN_BLOCK
    fwd = lambda i: (i, 0, 0)
    bwd = lambda i: (jnp.where(i < ctx_blk, ctx_blk - 1 - i, nblk - 1 + ctx_blk - i), 0, 0)
    blk = (S5_SCAN_BLOCK, rows, lanes)
    shape = jax.ShapeDtypeStruct(xf.shape, F32)
    return pl.pallas_call(
        _s5_scan_kernel,
        grid=(nblk,),
        in_specs=[pl.BlockSpec(blk, fwd), pl.BlockSpec(blk, bwd),
                  pl.BlockSpec((4, rows, lanes), lambda i: (0, 0, 0))],
        out_specs=[pl.BlockSpec(blk, fwd), pl.BlockSpec(blk, bwd)],
        out_shape=[shape, shape],
        scratch_shapes=[pltpu.VMEM((rows, lanes), F32), pltpu.VMEM((rows, lanes), F32)],
        compiler_params=_params(("arbitrary",), 32),
        name="s5_chunk_scan",
    )(xf, xb, a16)


def _s5_out_kernel(pf_ref, pb_ref, u_ref, wc_ref, y_ref):
    dot = functools.partial(jnp.dot, precision=HIGHEST, preferred_element_type=F32)
    y = (dot(pf_ref[...], wc_ref[0, 0:2 * S5_P, :])
         + dot(pb_ref[...], wc_ref[0, 2 * S5_P:4 * S5_P, :])
         + dot(u_ref[0], wc_ref[0, 4 * S5_P:, :]))
    y_ref[0] = jax.nn.gelu(y)


def _s5_out(pf, pb, ug, wc):
    rows = ug.shape[1]
    lh = S5_L * S5_H
    col = pl.BlockSpec((rows, 2 * S5_P), lambda g: (0, g))
    return pl.pallas_call(
        _s5_out_kernel,
        grid=(S5_G,),
        in_specs=[col, col,
                  pl.BlockSpec((1, rows, lh), lambda g: (g, 0, 0)),
                  pl.BlockSpec((1, 4 * S5_P + lh, lh), lambda g: (g, 0, 0))],
        out_specs=pl.BlockSpec((1, rows, lh), lambda g: (g, 0, 0)),
        out_shape=jax.ShapeDtypeStruct((S5_G, rows, lh), F32),
        compiler_params=_params(("parallel",), 32),
        name="s5_output",
    )(pf, pb, ug, wc)


def _s5_glu_kernel(y_ref, w_ref, o_ref):
    y = y_ref[...]
    gate = jnp.dot(y.astype(BF16), w_ref[...], preferred_element_type=F32)
    o_ref[...] = (y * jax.nn.sigmoid(gate)).astype(o_ref.dtype)


def _s5_glu(y, w_glu, tm=1024):
    m = y.shape[0]
    return pl.pallas_call(
        _s5_glu_kernel,
        grid=(m // tm,),
        in_specs=[pl.BlockSpec((tm, D_A), lambda i: (i, 0)),
                  pl.BlockSpec((D_A, D_A), lambda i: (0, 0))],
        out_specs=pl.BlockSpec((tm, D_A), lambda i: (i, 0)),
        out_shape=jax.ShapeDtypeStruct((m, D_A), BF16),
        compiler_params=_params(("parallel",), 32),
        name="s5_glu",
    )(y, w_glu)


def _s5_branch(z, lam_re, lam_im, log_dt, b_re, b_im, c_re, c_im, d_skip, w_glu_bf16):
    wst, wc, a16 = _s5_prep(lam_re, lam_im, log_dt, b_re, b_im, c_re, c_im, d_skip)
    nc_ctx, nc_lat = CTX_LEN // S5_L, SEQ // S5_L
    n_chunks = nc_ctx + nc_lat
    u = z[:, :D_A]
    u_ctx = u[N_LAT_ROWS:].reshape(BATCH, nc_ctx, S5_L, S5_G, S5_H).transpose(3, 1, 0, 2, 4)
    u_lat = u[:N_LAT_ROWS].reshape(BATCH, GRID_W // S5_L, S5_L, GRID_ROWS, S5_G, S5_H)
    u_lat = u_lat.transpose(4, 3, 1, 0, 2, 5).reshape(S5_G, nc_lat, BATCH, S5_L, S5_H)
    ug = jnp.concatenate([u_ctx, u_lat], axis=1).reshape(S5_G, n_chunks * BATCH, S5_L * S5_H)
    xf, xb = _s5_local(ug, wst)
    as_scan = lambda x: x.reshape(n_chunks, BATCH * S5_G, 2 * S5_P)
    a16_rows = jnp.tile(jnp.swapaxes(a16, 0, 1), (1, BATCH, 1))
    pf, pb = _s5_scan(as_scan(xf), as_scan(xb), a16_rows)
    as_cols = lambda x: x.reshape(n_chunks * BATCH, S5_G * 2 * S5_P)
    yg = _s5_out(as_cols(pf), as_cols(pb), ug, wc)
    y_lat = yg.reshape(S5_G, n_chunks, BATCH, S5_L, S5_H)[:, nc_ctx:]
    y_lat = y_lat.reshape(S5_G, GRID_ROWS, GRID_W // S5_L, BATCH, S5_L, S5_H)
    y_lat = y_lat.transpose(3, 2, 4, 1, 0, 5).reshape(N_LAT_ROWS, D_A)
    return _s5_glu(y_lat, w_glu_bf16)


HG_PREP_ROWS = 256
HG_UNROLL = 6


def _chunk_cumsum(x, reverse):
    n = x.shape[0]
    pos = lax.broadcasted_iota(jnp.int32, (n, 1), 0) & (HG_CHUNK - 1)
    s = 1
    while s < HG_CHUNK:
        if reverse:
            x = x + jnp.where(pos < HG_CHUNK - s, pltpu.roll(x, n - s, axis=0), 0.0)
        else:
            x = x + jnp.where(pos >= s, pltpu.roll(x, s, axis=0), 0.0)
        s *= 2
    return x


def _hgrn_kernel(qc_ref, ql_ref, vc_ref, vl_ref, ffc_ref, ffl_ref, fbc_ref, fbl_ref, og_ref, lg_ref,
                 gn_ref, o_ref, qe_s, ke_s, qb_s, kd_s, v_s, dec_s, of_s, ob_s, *, layer):
    c = HG_CHUNK
    n_chunks = T_ALL // c
    n_ctx_chunks = CTX_LEN // c
    row = lax.broadcasted_iota(jnp.int32, (c, c), 0)
    col = lax.broadcasted_iota(jnp.int32, (c, c), 1)
    causal = row >= col
    anti = row <= col
    nt = (((1,), (1,)), ((), ()))
    tn = (((0,), (0,)), ((), ()))

    def lower_bound(d):
        logits = [lg_ref[d, j, 0] for j in range(lg_ref.shape[1])]
        top = functools.reduce(jnp.maximum, logits)
        e = [jnp.exp(l - top) for l in logits]
        return sum(e[:layer + 1]) / sum(e)

    lb = (lower_bound(0), lower_bound(1))

    def prepare(row0, chunk0, q, v, fpre):
        n = q.shape[0]
        rows = pl.ds(row0, n)
        as_chunks = lambda a: a.reshape(n // c, c, HG_DK)
        flat_bf16 = lambda a: a.reshape(n, HG_DK).astype(BF16)
        v_s[rows, :] = v.astype(BF16)
        q3 = as_chunks(q)
        for d in range(2):
            f = lb[d] + (1.0 - lb[d]) * jax.nn.sigmoid(fpre[d])
            k3 = as_chunks(1.0 - f)
            b3 = as_chunks(_chunk_cumsum(jnp.log(f), reverse=d == 1))
            mid = c // 2 - 1 if d == 0 else c // 2
            last = c - 1 if d == 0 else 0
            b_mid, b_last = b3[:, mid:mid + 1, :], b3[:, last:last + 1, :]
            qe_s[d, rows, :] = flat_bf16(q3 * jnp.exp(b3 - b_mid))
            ke_s[d, rows, :] = flat_bf16(k3 * jnp.exp(b_mid - b3))
            qb_s[d, rows, :] = flat_bf16(q3 * jnp.exp(b3))
            kd_s[d, rows, :] = flat_bf16(k3 * jnp.exp(b_last - b3))
            dec_s[d, pl.ds(chunk0, n // c)] = jnp.exp(b_last)

    prepare(0, 0, qc_ref[...], vc_ref[...], (ffc_ref[...], fbc_ref[...]))

    def prep_body(blk, carry):
        r0 = pl.multiple_of(blk * HG_PREP_ROWS, HG_PREP_ROWS)
        src = pl.ds(r0, HG_PREP_ROWS)
        prepare(pl.multiple_of(CTX_LEN + r0, HG_PREP_ROWS), n_ctx_chunks + blk * (HG_PREP_ROWS // c),
                ql_ref[src, :], vl_ref[src, :], (ffl_ref[src, :], fbl_ref[src, :]))
        return carry

    lax.fori_loop(0, SEQ // HG_PREP_ROWS, prep_body, 0)

    def chunk(cidx, d, state_t):
        rows = pl.ds(pl.multiple_of(cidx * c, c), c)
        v = v_s[rows, :]
        scores = lax.dot_general(qe_s[d, rows, :], ke_s[d, rows, :], nt, preferred_element_type=F32)
        scores = jnp.where(causal if d == 0 else anti, scores, 0.0)
        o = jnp.dot(scores.astype(BF16), v, preferred_element_type=F32)
        o = o + lax.dot_general(qb_s[d, rows, :], state_t.astype(BF16), nt, preferred_element_type=F32)
        kv_t = lax.dot_general(v, kd_s[d, rows, :], tn, preferred_element_type=F32)
        (of_s if d == 0 else ob_s)[rows, :] = o
        return state_t * dec_s[d, cidx] + kv_t

    def body(i, carry):
        sf, sb = carry
        sf = chunk(i, 0, sf)
        jb = jnp.where(i < n_ctx_chunks, n_ctx_chunks - 1 - i, n_chunks - 1 + n_ctx_chunks - i)
        sb = chunk(jb, 1, sb)
        return sf, sb

    zero = jnp.zeros((HG_DK, HG_DK), F32)
    lax.fori_loop(0, n_chunks, body, (zero, zero), unroll=HG_UNROLL)

    o = of_s[CTX_LEN:, :] + ob_s[CTX_LEN:, :]
    o = o * lax.rsqrt(jnp.mean(o * o, axis=-1, keepdims=True) + EPS)
    o_ref[...] = (o * gn_ref[0] * jax.nn.silu(og_ref[...])).astype(o_ref.dtype)


def _hgrn_branch(z, lb_logits, hg_norm, layer):
    n_slots = lb_logits.shape[1]
    ctx_blk0 = N_LAT_ROWS // CTX_LEN

    def windows(i):
        col_blk = (D_A + i * D_B) // HG_DK
        return [pl.BlockSpec((CTX_LEN, HG_DK), lambda b, h: (ctx_blk0 + b, col_blk + h)),
                pl.BlockSpec((SEQ, HG_DK), lambda b, h: (b, col_blk + h))]

    in_specs = [spec for i in range(4) for spec in windows(i)]
    in_specs += [windows(4)[1],
                 pl.BlockSpec((2, n_slots, 1, 1, HG_DK), lambda b, h: (0, 0, h, 0, 0)),
                 pl.BlockSpec((1, 1, HG_DK), lambda b, h: (h, 0, 0))]
    operand = lambda: pltpu.VMEM((2, T_ALL, HG_DK), BF16)
    lg = lb_logits.astype(F32).reshape(2, n_slots, HG_HEADS, 1, HG_DK)
    return pl.pallas_call(
        functools.partial(_hgrn_kernel, layer=layer),
        grid=(BATCH, HG_HEADS),
        in_specs=in_specs,
        out_specs=pl.BlockSpec((SEQ, HG_DK), lambda b, h: (b, h)),
        out_shape=jax.ShapeDtypeStruct((N_LAT_ROWS, D_B), BF16),
        scratch_shapes=[operand(), operand(), operand(), operand(),
                        pltpu.VMEM((T_ALL, HG_DK), BF16),
                        pltpu.VMEM((2, T_ALL // HG_CHUNK, 1, HG_DK), F32),
                        pltpu.VMEM((T_ALL, HG_DK), F32),
                        pltpu.VMEM((T_ALL, HG_DK), F32)],
        compiler_params=_params(("parallel", "parallel"), 40),
        name="hgrn2",
    )(*([z] * 9), lg, hg_norm.reshape(HG_HEADS, 1, HG_DK))


def _pad_ff_cols(w):
    return jnp.pad(w.astype(BF16), ((0, 0), (0, D_FF_PAD - D_FF)))


def _pad_ff_rows(w):
    return jnp.pad(w.astype(BF16), ((0, D_FF_PAD - D_FF), (0, 0)))


def kernel(x, c, ctx, c_ctx, w_ada, b_ada, norm_ffn1, w1_ffn1, w3_ffn1, w2_ffn1, norm_mix, w_in,
           s5_lam_re, s5_lam_im, s5_log_dt, s5_b_re, s5_b_im, s5_c_re, s5_c_im, s5_d, s5_w_glu,
           hg_lb_logits, hg_norm, w_proj_a, w_proj_b, w_out, norm_ffn2, w1_ffn2, w3_ffn2, w2_ffn2,
           norm_final):
    layer = 0
    c8 = jnp.concatenate([c, c_ctx[None], jnp.zeros((8 - BATCH - 1, D_MODEL), F32)], axis=0)
    mod = _ada(c8, w_ada[layer], b_ada[layer]).reshape(8, N_MOD, D_MODEL)
    mod = jnp.concatenate([mod[BATCH:BATCH + 1], mod[:BATCH]], axis=0)
    m = [mod[:, i].reshape(BATCH + 1, 1, D_MODEL) for i in range(N_MOD)]

    x_scan = x.reshape(BATCH, GRID_ROWS, GRID_W, D_MODEL).swapaxes(1, 2).reshape(N_LAT_ROWS, D_MODEL)
    h0 = jnp.concatenate([x_scan, ctx.reshape(N_CTX_ROWS, D_MODEL)], axis=0)
    tile = 1024

    z1 = _norm_mod(h0, norm_ffn1[layer], m[0], m[1], _group_full(512))
    g1 = _swiglu_up(z1, _pad_ff_cols(w1_ffn1[layer]), _pad_ff_cols(w3_ffn1[layer]))
    h1 = _mm_res(g1, _pad_ff_rows(w2_ffn1[layer]), h0, 0, m[2], 0.5, _group_full(tile),
                 tile, 1024, D_FF_PAD // 4, 52)

    z2 = _norm_mod(h1, norm_mix[layer], m[3], m[4], _group_full(512))
    z = _mm(z2, w_in[layer].astype(BF16), F32, tile, 1024, 52, "mixer_in_proj")

    y_a = _s5_branch(z, s5_lam_re[layer], s5_lam_im[layer], s5_log_dt[layer], s5_b_re[layer],
                     s5_b_im[layer], s5_c_re[layer], s5_c_im[layer], s5_d[layer],
                     s5_w_glu[layer].astype(BF16))
    y_b = _hgrn_branch(z, hg_lb_logits, hg_norm[layer], layer)

    ga_tile = (D_A + 5 * D_B) // 512
    merged = _merge(y_a, w_proj_a[layer].astype(BF16), y_b, w_proj_b[layer].astype(BF16), z,
                    ga_tile, ga_tile + D_MODEL // 512, 0)
    h2 = _mm_res(merged, w_out[layer].astype(BF16), h1, 0, m[5], 1.0, _group_lat(tile),
                 tile, 512, D_MODEL, 48)

    z3 = _norm_mod(h2, norm_ffn2[layer], m[6], m[7], _group_lat(512))
    g2 = _swiglu_up(z3, _pad_ff_cols(w1_ffn2[layer]), _pad_ff_cols(w3_ffn2[layer]))
    h3 = _mm_res(g2, _pad_ff_rows(w2_ffn2[layer]), h2, 0, m[8], 0.5, _group_lat(tile),
                 tile, 1024, D_FF_PAD // 4, 52)
    out = _final_norm(h3, norm_final).reshape(BATCH, GRID_W, GRID_ROWS, D_MODEL)
    return out.swapaxes(1, 2).reshape(BATCH, SEQ, D_MODEL)
```

```python
import functools
import math

import jax
import jax.numpy as jnp
from jax import lax
from jax.experimental import pallas as pl
from jax.experimental.pallas import tpu as pltpu

F32 = jnp.float32
BF16 = jnp.bfloat16
HIGHEST = lax.Precision.HIGHEST

D_MODEL = 4096
BATCH = 4
SEQ = 2048
GRID_W = 64
GRID_ROWS = SEQ // GRID_W
CTX_LEN = 256
T_ALL = CTX_LEN + SEQ
D_FF = 11008
D_A = 1024
S5_H = 16
S5_G = D_A // S5_H
S5_P = 64
S5_L = 16
D_B = 2048
HG_DK = 128
HG_HEADS = D_B // HG_DK
HG_CHUNK = 64
N_MOD = 9
N_IN = D_A + 5 * D_B + 2 * D_MODEL
EPS = 1e-6

N_CTX_ROWS = BATCH * CTX_LEN
N_LAT_ROWS = BATCH * SEQ
N_ROWS = N_CTX_ROWS + N_LAT_ROWS
MIB = 1024 * 1024


def _params(semantics, vmem_mib):
    return pltpu.CompilerParams(dimension_semantics=semantics, vmem_limit_bytes=vmem_mib * MIB)


def _group_full(tm):
    n_lat_tiles, per_batch = N_LAT_ROWS // tm, SEQ // tm
    return lambda i: jnp.where(i < n_lat_tiles, 1 + i // per_batch, 0)


def _group_lat(tm):
    per_batch = SEQ // tm
    return lambda i: 1 + i // per_batch


def _ada_kernel(c_ref, w_ref, b_ref, o_ref):
    sc = jax.nn.silu(c_ref[...]).astype(BF16)
    o_ref[...] = jnp.dot(sc, w_ref[...].astype(BF16), preferred_element_type=F32) + b_ref[...]


def _ada(c8, w_ada, b_ada):
    n = w_ada.shape[1]
    tn = 512
    return pl.pallas_call(
        _ada_kernel,
        grid=(n // tn,),
        in_specs=[pl.BlockSpec((8, D_MODEL), lambda j: (0, 0)),
                  pl.BlockSpec((D_MODEL, tn), lambda j: (0, j)),
                  pl.BlockSpec((1, tn), lambda j: (0, j))],
        out_specs=pl.BlockSpec((8, tn), lambda j: (0, j)),
        out_shape=jax.ShapeDtypeStruct((8, n), F32),
        compiler_params=_params(("parallel",), 40),
        name="ada_table",
    )(c8, w_ada, b_ada.reshape(1, n))


def _norm_mod_kernel(x_ref, g_ref, sh_ref, sc_ref, o_ref):
    x = x_ref[...]
    var = jnp.mean(x * x, axis=-1, keepdims=True)
    y = x * lax.rsqrt(var + EPS) * g_ref[...]
    o_ref[...] = (y * (1.0 + sc_ref[0]) + sh_ref[0]).astype(o_ref.dtype)


def _norm_mod(x, gain, shift, scale, group_fn, tm=512):
    rows = x.shape[0]
    vec = pl.BlockSpec((1, 1, D_MODEL), lambda i: (group_fn(i), 0, 0))
    return pl.pallas_call(
        _norm_mod_kernel,
        grid=(rows // tm,),
        in_specs=[pl.BlockSpec((tm, D_MODEL), lambda i: (i, 0)),
                  pl.BlockSpec((1, D_MODEL), lambda i: (0, 0)),
                  vec, vec],
        out_specs=pl.BlockSpec((tm, D_MODEL), lambda i: (i, 0)),
        out_shape=jax.ShapeDtypeStruct((rows, D_MODEL), BF16),
        compiler_params=_params(("parallel",), 40),
        name="norm_modulate",
    )(x, gain.reshape(1, D_MODEL), shift, scale)


def _final_norm_kernel(x_ref, g_ref, o_ref):
    x = x_ref[...]
    var = jnp.mean(x * x, axis=-1, keepdims=True)
    o_ref[...] = x * lax.rsqrt(var + EPS) * g_ref[...]


def _final_norm(x, gain, tm=512):
    rows = x.shape[0]
    return pl.pallas_call(
        _final_norm_kernel,
        grid=(rows // tm,),
        in_specs=[pl.BlockSpec((tm, D_MODEL), lambda i: (i, 0)),
                  pl.BlockSpec((1, D_MODEL), lambda i: (0, 0))],
        out_specs=pl.BlockSpec((tm, D_MODEL), lambda i: (i, 0)),
        out_shape=jax.ShapeDtypeStruct((rows, D_MODEL), F32),
        compiler_params=_params(("parallel",), 40),
        name="final_norm",
    )(x, gain.reshape(1, D_MODEL))


def _swiglu_up_kernel(z_ref, w1_ref, w3_ref, o_ref):
    z = z_ref[...]
    h1 = jnp.dot(z, w1_ref[...].astype(BF16), preferred_element_type=F32)
    h3 = jnp.dot(z, w3_ref[...].astype(BF16), preferred_element_type=F32)
    o_ref[...] = (jax.nn.silu(h1) * h3).astype(o_ref.dtype)


def _swiglu_up(z, w1, w3, tm=1024, tn=256):
    m, k = z.shape
    n = w1.shape[1]
    return pl.pallas_call(
        _swiglu_up_kernel,
        grid=(m // tm, n // tn),
        in_specs=[pl.BlockSpec((tm, k), lambda i, j: (i, 0)),
                  pl.BlockSpec((k, tn), lambda i, j: (0, j)),
                  pl.BlockSpec((k, tn), lambda i, j: (0, j))],
        out_specs=pl.BlockSpec((tm, tn), lambda i, j: (i, j)),
        out_shape=jax.ShapeDtypeStruct((m, n), BF16),
        compiler_params=_params(("parallel", "arbitrary"), 52),
        name="swiglu_up",
    )(z, w1, w3)


def _mm_res_kernel(a_ref, w_ref, res_ref, gate_ref, o_ref, acc_ref, *, scale, nk):
    k = pl.program_id(2)

    @pl.when(k == 0)
    def _():
        acc_ref[...] = jnp.zeros_like(acc_ref)

    acc_ref[...] += jnp.dot(a_ref[...], w_ref[...].astype(BF16), preferred_element_type=F32)

    @pl.when(k == nk - 1)
    def _():
        o_ref[...] = res_ref[...] + (scale * gate_ref[0]) * acc_ref[...]


def _mm_res(a, w, res, res_tile_off, gate, scale, group_fn, tm, tn, tk, vmem_mib):
    m, k = a.shape
    n = w.shape[1]
    nk = k // tk
    return pl.pallas_call(
        functools.partial(_mm_res_kernel, scale=scale, nk=nk),
        grid=(m // tm, n // tn, nk),
        in_specs=[pl.BlockSpec((tm, tk), lambda i, j, kk: (i, kk)),
                  pl.BlockSpec((tk, tn), lambda i, j, kk: (kk, j)),
                  pl.BlockSpec((tm, tn), lambda i, j, kk: (i + res_tile_off, j)),
                  pl.BlockSpec((1, 1, tn), lambda i, j, kk: (group_fn(i), 0, j))],
        out_specs=pl.BlockSpec((tm, tn), lambda i, j, kk: (i, j)),
        out_shape=jax.ShapeDtypeStruct((m, n), F32),
        scratch_shapes=[pltpu.VMEM((tm, tn), F32)],
        compiler_params=_params(("parallel", "parallel", "arbitrary"), vmem_mib),
        name="matmul_gated_residual",
    )(a, w, res, gate)


def _mm_kernel(a_ref, w_ref, o_ref):
    o_ref[...] = jnp.dot(a_ref[...], w_ref[...].astype(BF16), preferred_element_type=F32).astype(o_ref.dtype)


def _mm(a, w, out_dtype, tm, tn, vmem_mib, name):
    m, k = a.shape
    n = w.shape[1]
    return pl.pallas_call(
        _mm_kernel,
        grid=(m // tm, n // tn),
        in_specs=[pl.BlockSpec((tm, k), lambda i, j: (i, 0)),
                  pl.BlockSpec((k, tn), lambda i, j: (0, j))],
        out_specs=pl.BlockSpec((tm, tn), lambda i, j: (i, j)),
        out_shape=jax.ShapeDtypeStruct((m, n), out_dtype),
        compiler_params=_params(("parallel", "arbitrary"), vmem_mib),
        name=name,
    )(a, w)


def _merge_kernel(ya_ref, wa_ref, yb_ref, wb_ref, ga_ref, gb_ref, o_ref):
    pa = jnp.dot(ya_ref[...], wa_ref[...].astype(BF16), preferred_element_type=F32)
    pb = jnp.dot(yb_ref[...], wb_ref[...].astype(BF16), preferred_element_type=F32)
    o_ref[...] = (jax.nn.sigmoid(ga_ref[...]) * pa + jax.nn.sigmoid(gb_ref[...]) * pb).astype(o_ref.dtype)


def _merge(ya, wa, yb, wb, z, ga_col_tile, gb_col_tile, z_row_tile_off, tm=1024, tn=512):
    m = ya.shape[0]
    n = wa.shape[1]
    return pl.pallas_call(
        _merge_kernel,
        grid=(m // tm, n // tn),
        in_specs=[pl.BlockSpec((tm, D_A), lambda i, j: (i, 0)),
                  pl.BlockSpec((D_A, tn), lambda i, j: (0, j)),
                  pl.BlockSpec((tm, D_B), lambda i, j: (i, 0)),
                  pl.BlockSpec((D_B, tn), lambda i, j: (0, j)),
                  pl.BlockSpec((tm, tn), lambda i, j: (i + z_row_tile_off, ga_col_tile + j)),
                  pl.BlockSpec((tm, tn), lambda i, j: (i + z_row_tile_off, gb_col_tile + j))],
        out_specs=pl.BlockSpec((tm, tn), lambda i, j: (i, j)),
        out_shape=jax.ShapeDtypeStruct((m, n), BF16),
        compiler_params=_params(("parallel", "arbitrary"), 48),
        name="gated_merge",
    )(ya, wa, yb, wb, z, z)


def _s5_prep_kernel(ldt_ref, lam_row_ref, lam_col_ref, b1_ref, b2_ref, c1_ref, c2_ref, dt_ref,
                    wst_ref, wc_ref, a16_ref):
    g = pl.program_id(0)
    lh = S5_L * S5_H
    lane_sign = jnp.where(lax.broadcasted_iota(jnp.int32, (1, 2 * S5_P), 1) < S5_P, -1.0, 1.0)
    row_sign = jnp.where(lax.broadcasted_iota(jnp.int32, (2 * S5_P, 1), 0) < S5_P, 1.0, -1.0)
    log2_h = S5_H.bit_length() - 1
    s_of_row = lax.shift_right_logical(lax.broadcasted_iota(jnp.int32, (lh, 1), 0), log2_h).astype(F32)
    t_of_col = lax.shift_right_logical(lax.broadcasted_iota(jnp.int32, (1, lh), 1), log2_h).astype(F32)
    b1 = b1_ref[0]
    b2 = b2_ref[0]
    c1 = c1_ref[0]
    c2 = c2_ref[0]
    toep = None
    for d in range(2):
        dt = jnp.exp(jnp.full((1, 1), ldt_ref[d, g], F32))
        lr = lam_row_ref[d, 0]
        li = lam_row_ref[2 + d, 0]
        mag = jnp.exp(lr * dt)
        ab_re, ab_im = mag * jnp.cos(li * dt), mag * jnp.sin(li * dt)
        den = lr * lr + li * li
        nr = ab_re - 1.0
        kr = (nr * lr + ab_im * li) / den
        ki = (ab_im * lr - nr * li) / den
        bk = kr * b1 + lane_sign * ki * b2
        bk_sw = lane_sign * (kr * b2 - lane_sign * ki * b1)
        tau = (S5_L - 1.0) - s_of_row if d == 0 else s_of_row
        pm = jnp.exp(tau * (lr * dt))
        p_re, p_im = pm * jnp.cos(tau * (li * dt)), pm * jnp.sin(tau * (li * dt))
        wst_ref[0, :, d * 2 * S5_P:(d + 1) * 2 * S5_P] = p_re * bk + p_im * bk_sw
        m16 = jnp.exp(S5_L * (lr * dt))
        a_re, a_im = m16 * jnp.cos(S5_L * (li * dt)), m16 * jnp.sin(S5_L * (li * dt))
        a16_ref[0, 2 * d:2 * d + 1, :] = a_re
        a16_ref[0, 2 * d + 1:2 * d + 2, :] = lane_sign * a_im
        lrc = lam_col_ref[d, 0]
        lic = lam_col_ref[2 + d, 0]

        def c_pow(tau_c):
            qm = jnp.exp(tau_c * (lrc * dt))
            q_re, q_im = qm * jnp.cos(tau_c * (lic * dt)), qm * jnp.sin(tau_c * (lic * dt))
            return row_sign * c1 * q_re - c2 * q_im

        wc_ref[0, d * 2 * S5_P:(d + 1) * 2 * S5_P, :] = c_pow(t_of_col + 1.0 if d == 0 else S5_L - t_of_col)
        m_all = jnp.dot(bk[0:S5_H], c_pow(t_of_col if d == 0 else (S5_L - 1.0) - t_of_col),
                        precision=HIGHEST, preferred_element_type=F32)
        lane = lax.broadcasted_iota(jnp.int32, (S5_H, lh), 1)
        blocks = []
        for s in range(S5_L):
            if d == 0:
                shift, keep = S5_H * s, lane >= S5_H * s
            else:
                shift, keep = (S5_H * (s + 1)) % lh, lane < S5_H * (s + 1)
            rolled = m_all if shift == 0 else pltpu.roll(m_all, shift, axis=1)
            blocks.append(jnp.where(keep, rolled, 0.0))
        td = jnp.concatenate(blocks, axis=0)
        toep = td if toep is None else toep + td
    diag = lax.broadcasted_iota(jnp.int32, (lh, lh), 0) == lax.broadcasted_iota(jnp.int32, (lh, lh), 1)
    wc_ref[0, 4 * S5_P:, :] = toep + jnp.where(diag, dt_ref[0], 0.0)


def _s5_prep(lam_re, lam_im, log_dt, b_re, b_im, c_re, c_im, d_skip):
    lh = S5_L * S5_H
    dup = lambda z: jnp.concatenate([z, z], axis=-1)
    lam = jnp.concatenate([lam_re, lam_im], axis=0)
    lam_row = dup(lam)[:, :, None, :]
    lam_col = dup(lam)[:, :, :, None]
    bt_re, bt_im = jnp.swapaxes(b_re, 1, 2), jnp.swapaxes(b_im, 1, 2)
    b1 = jnp.tile(jnp.concatenate([bt_re, bt_im], axis=-1), (1, S5_L, 1))
    b2 = jnp.tile(jnp.concatenate([bt_im, bt_re], axis=-1), (1, S5_L, 1))
    ct_re, ct_im = jnp.swapaxes(c_re, 1, 2), jnp.swapaxes(c_im, 1, 2)
    c1 = jnp.tile(jnp.concatenate([ct_re, ct_im], axis=1), (1, 1, S5_L))
    c2 = jnp.tile(jnp.concatenate([ct_im, ct_re], axis=1), (1, 1, S5_L))
    d_tiled = jnp.tile(d_skip.reshape(S5_G, 1, S5_H), (1, 1, S5_L))
    grp = lambda shape: pl.BlockSpec((1,) + shape, lambda g: (g, 0, 0))
    return pl.pallas_call(
        _s5_prep_kernel,
        grid=(S5_G,),
        in_specs=[pl.BlockSpec(memory_space=pltpu.SMEM),
                  pl.BlockSpec((4, 1, 1, 2 * S5_P), lambda g: (0, g, 0, 0)),
                  pl.BlockSpec((4, 1, 2 * S5_P, 1), lambda g: (0, g, 0, 0)),
                  grp((lh, 2 * S5_P)), grp((lh, 2 * S5_P)),
                  grp((2 * S5_P, lh)), grp((2 * S5_P, lh)),
                  grp((1, lh))],
        out_specs=[grp((lh, 4 * S5_P)), grp((4 * S5_P + lh, lh)), grp((4, 2 * S5_P))],
        out_shape=[jax.ShapeDtypeStruct((S5_G, lh, 4 * S5_P), F32),
                   jax.ShapeDtypeStruct((S5_G, 4 * S5_P + lh, lh), F32),
                   jax.ShapeDtypeStruct((S5_G, 4, 2 * S5_P), F32)],
        compiler_params=_params(("parallel",), 32),
        name="s5_prep",
    )(log_dt, lam_row, lam_col, b1, b2, c1, c2, d_tiled)


def _s5_local_kernel(u_ref, wst_ref, xf_ref, xb_ref):
    r = jnp.dot(u_ref[0], wst_ref[0], precision=HIGHEST, preferred_element_type=F32)
    xf_ref[...] = r[:, :2 * S5_P]
    xb_ref[...] = r[:, 2 * S5_P:]


def _s5_local(ug, wst):
    rows = ug.shape[1]
    lh = S5_L * S5_H
    col = pl.BlockSpec((rows, 2 * S5_P), lambda g: (0, g))
    shape = jax.ShapeDtypeStruct((rows, S5_G * 2 * S5_P), F32)
    return pl.pallas_call(
        _s5_local_kernel,
        grid=(S5_G,),
        in_specs=[pl.BlockSpec((1, rows, lh), lambda g: (g, 0, 0)),
                  pl.BlockSpec((1, lh, 4 * S5_P), lambda g: (g, 0, 0))],
        out_specs=[col, col],
        out_shape=[shape, shape],
        compiler_params=_params(("parallel",), 32),
        name="s5_local_state",
    )(ug, wst)


S5_SCAN_BLOCK = 8


def _s5_scan_kernel(xf_ref, xb_ref, a_ref, pf_ref, pb_ref, sf_ref, sb_ref):
    @pl.when(pl.program_id(0) == 0)
    def _():
        sf_ref[...] = jnp.zeros_like(sf_ref)
        sb_ref[...] = jnp.zeros_like(sb_ref)

    def step(s, x, aa, ab):
        return aa * s + ab * pltpu.roll(s, S5_P, axis=1) + x

    s = sf_ref[...]
    for j in range(S5_SCAN_BLOCK):
        pf_ref[j] = s
        s = step(s, xf_ref[j], a_ref[0], a_ref[1])
    sf_ref[...] = s
    s = sb_ref[...]
    for j in reversed(range(S5_SCAN_BLOCK)):
        pb_ref[j] = s
        s = step(s, xb_ref[j], a_ref[2], a_ref[3])
    sb_ref[...] = s


def _s5_scan(xf, xb, a16):
    n_chunks, rows, lanes = xf.shape
    nblk = n_chunks // S5_SCAN_BLOCK
    ctx_blk = (CTX_LEN // S5_L) // S5_SCAN_BLOCK
    fwd = lambda i: (i, 0, 0)
    bwd = lambda i: (jnp.where(i < ctx_blk, ctx_blk - 1 - i, nblk - 1 + ctx_blk - i), 0, 0)
    blk = (S5_SCAN_BLOCK, rows, lanes)
    shape = jax.ShapeDtypeStruct(xf.shape, F32)
    return pl.pallas_call(
        _s5_scan_kernel,
        grid=(nblk,),
        in_specs=[pl.BlockSpec(blk, fwd), pl.BlockSpec(blk, bwd),
                  pl.BlockSpec((4, rows, lanes), lambda i: (0, 0, 0))],
        out_specs=[pl.BlockSpec(blk, fwd), pl.BlockSpec(blk, bwd)],
        out_shape=[shape, shape],
        scratch_shapes=[pltpu.VMEM((rows, lanes), F32), pltpu.VMEM((rows, lanes), F32)],
        compiler_params=_params(("arbitrary",), 32),
        name="s5_chunk_scan",
    )(xf, xb, a16)


def _s5_out_kernel(pf_ref, pb_ref, u_ref, wc_ref, y_ref):
    dot = functools.partial(jnp.dot, precision=HIGHEST, preferred_element_type=F32)
    y = (dot(pf_ref[...], wc_ref[0, 0:2 * S5_P, :])
         + dot(pb_ref[...], wc_ref[0, 2 * S5_P:4 * S5_P, :])
         + dot(u_ref[0], wc_ref[0, 4 * S5_P:, :]))
    y_ref[0] = jax.nn.gelu(y)


def _s5_out(pf, pb, ug, wc):
    rows = ug.shape[1]
    lh = S5_L * S5_H
    col = pl.BlockSpec((rows, 2 * S5_P), lambda g: (0, g))
    return pl.pallas_call(
        _s5_out_kernel,
        grid=(S5_G,),
        in_specs=[col, col,
                  pl.BlockSpec((1, rows, lh), lambda g: (g, 0, 0)),
                  pl.BlockSpec((1, 4 * S5_P + lh, lh), lambda g: (g, 0, 0))],
        out_specs=pl.BlockSpec((1, rows, lh), lambda g: (g, 0, 0)),
        out_shape=jax.ShapeDtypeStruct((S5_G, rows, lh), F32),
        compiler_params=_params(("parallel",), 32),
        name="s5_output",
    )(pf, pb, ug, wc)


def _s5_glu_kernel(y_ref, w_ref, o_ref):
    y = y_ref[...]
    gate = jnp.dot(y.astype(BF16), w_ref[...].astype(BF16), preferred_element_type=F32)
    o_ref[...] = (y * jax.nn.sigmoid(gate)).astype(o_ref.dtype)


def _s5_glu(y, w_glu, tm=1024):
    m = y.shape[0]
    return pl.pallas_call(
        _s5_glu_kernel,
        grid=(m // tm,),
        in_specs=[pl.BlockSpec((tm, D_A), lambda i: (i, 0)),
                  pl.BlockSpec((D_A, D_A), lambda i: (0, 0))],
        out_specs=pl.BlockSpec((tm, D_A), lambda i: (i, 0)),
        out_shape=jax.ShapeDtypeStruct((m, D_A), BF16),
        compiler_params=_params(("parallel",), 32),
        name="s5_glu",
    )(y, w_glu)


def _s5_branch(z, lam_re, lam_im, log_dt, b_re, b_im, c_re, c_im, d_skip, w_glu):
    wst, wc, a16 = _s5_prep(lam_re, lam_im, log_dt, b_re, b_im, c_re, c_im, d_skip)
    nc_ctx, nc_lat = CTX_LEN // S5_L, SEQ // S5_L
    n_chunks = nc_ctx + nc_lat
    u = z[:, :D_A]
    u_ctx = u[N_LAT_ROWS:].reshape(BATCH, nc_ctx, S5_L, S5_G, S5_H).transpose(3, 1, 0, 2, 4)
    u_lat = u[:N_LAT_ROWS].reshape(BATCH, GRID_W // S5_L, S5_L, GRID_ROWS, S5_G, S5_H)
    u_lat = u_lat.transpose(4, 3, 1, 0, 2, 5).reshape(S5_G, nc_lat, BATCH, S5_L, S5_H)
    ug = jnp.concatenate([u_ctx, u_lat], axis=1).reshape(S5_G, n_chunks * BATCH, S5_L * S5_H)
    xf, xb = _s5_local(ug, wst)
    as_scan = lambda x: x.reshape(n_chunks, BATCH * S5_G, 2 * S5_P)
    a16_rows = jnp.tile(jnp.swapaxes(a16, 0, 1), (1, BATCH, 1))
    pf, pb = _s5_scan(as_scan(xf), as_scan(xb), a16_rows)
    as_cols = lambda x: x.reshape(n_chunks * BATCH, S5_G * 2 * S5_P)
    yg = _s5_out(as_cols(pf), as_cols(pb), ug, wc)
    y_lat = yg.reshape(S5_G, n_chunks, BATCH, S5_L, S5_H)[:, nc_ctx:]
    y_lat = y_lat.reshape(S5_G, GRID_ROWS, GRID_W // S5_L, BATCH, S5_L, S5_H)
    y_lat = y_lat.transpose(3, 2, 4, 1, 0, 5).reshape(N_LAT_ROWS, D_A)
    return _s5_glu(y_lat, w_glu)


HG_PREP_ROWS = 256
HG_UNROLL = 6


def _chunk_cumsum(x, reverse):
    n = x.shape[0]
    pos = lax.broadcasted_iota(jnp.int32, (n, 1), 0) & (HG_CHUNK - 1)
    s = 1
    while s < HG_CHUNK:
        if reverse:
            x = x + jnp.where(pos < HG_CHUNK - s, pltpu.roll(x, n - s, axis=0), 0.0)
        else:
            x = x + jnp.where(pos >= s, pltpu.roll(x, s, axis=0), 0.0)
        s *= 2
    return x


def _hgrn_kernel(qc_ref, ql_ref, vc_ref, vl_ref, ffc_ref, ffl_ref, fbc_ref, fbl_ref, og_ref, lg_ref,
                 gn_ref, o_ref, qe_s, ke_s, qb_s, kd_s, v_s, dec_s, of_s, ob_s, *, layer):
    c = HG_CHUNK
    n_chunks = T_ALL // c
    n_ctx_chunks = CTX_LEN // c
    row = lax.broadcasted_iota(jnp.int32, (c, c), 0)
    col = lax.broadcasted_iota(jnp.int32, (c, c), 1)
    causal = row >= col
    anti = row <= col
    nt = (((1,), (1,)), ((), ()))
    tn = (((0,), (0,)), ((), ()))

    def lower_bound(d):
        logits = [lg_ref[d, j, 0] for j in range(lg_ref.shape[1])]
        top = functools.reduce(jnp.maximum, logits)
        e = [jnp.exp(l - top) for l in logits]
        return sum(e[:layer + 1]) / sum(e)

    lb = (lower_bound(0), lower_bound(1))

    def prepare(row0, chunk0, q, v, fpre):
        n = q.shape[0]
        rows = pl.ds(row0, n)
        as_chunks = lambda a: a.reshape(n // c, c, HG_DK)
        flat_bf16 = lambda a: a.reshape(n, HG_DK).astype(BF16)
        v_s[rows, :] = v.astype(BF16)
        q3 = as_chunks(q)
        for d in range(2):
            f = lb[d] + (1.0 - lb[d]) * jax.nn.sigmoid(fpre[d])
            k3 = as_chunks(1.0 - f)
            b3 = as_chunks(_chunk_cumsum(jnp.log(f), reverse=d == 1))
            mid = c // 2 - 1 if d == 0 else c // 2
            last = c - 1 if d == 0 else 0
            b_mid, b_last = b3[:, mid:mid + 1, :], b3[:, last:last + 1, :]
            qe_s[d, rows, :] = flat_bf16(q3 * jnp.exp(b3 - b_mid))
            ke_s[d, rows, :] = flat_bf16(k3 * jnp.exp(b_mid - b3))
            qb_s[d, rows, :] = flat_bf16(q3 * jnp.exp(b3))
            kd_s[d, rows, :] = flat_bf16(k3 * jnp.exp(b_last - b3))
            dec_s[d, pl.ds(chunk0, n // c)] = jnp.exp(b_last)

    prepare(0, 0, qc_ref[...], vc_ref[...], (ffc_ref[...], fbc_ref[...]))

    def prep_body(blk, carry):
        r0 = pl.multiple_of(blk * HG_PREP_ROWS, HG_PREP_ROWS)
        src = pl.ds(r0, HG_PREP_ROWS)
        prepare(pl.multiple_of(CTX_LEN + r0, HG_PREP_ROWS), n_ctx_chunks + blk * (HG_PREP_ROWS // c),
                ql_ref[src, :], vl_ref[src, :], (ffl_ref[src, :], fbl_ref[src, :]))
        return carry

    lax.fori_loop(0, SEQ // HG_PREP_ROWS, prep_body, 0)

    def chunk(cidx, d, state_t):
        rows = pl.ds(pl.multiple_of(cidx * c, c), c)
        v = v_s[rows, :]
        scores = lax.dot_general(qe_s[d, rows, :], ke_s[d, rows, :], nt, preferred_element_type=F32)
        scores = jnp.where(causal if d == 0 else anti, scores, 0.0)
        o = jnp.dot(scores.astype(BF16), v, preferred_element_type=F32)
        o = o + lax.dot_general(qb_s[d, rows, :], state_t.astype(BF16), nt, preferred_element_type=F32)
        kv_t = lax.dot_general(v, kd_s[d, rows, :], tn, preferred_element_type=F32)
        (of_s if d == 0 else ob_s)[rows, :] = o
        return state_t * dec_s[d, cidx] + kv_t

    def body(i, carry):
        sf, sb = carry
        sf = chunk(i, 0, sf)
        jb = jnp.where(i < n_ctx_chunks, n_ctx_chunks - 1 - i, n_chunks - 1 + n_ctx_chunks - i)
        sb = chunk(jb, 1, sb)
        return sf, sb

    zero = jnp.zeros((HG_DK, HG_DK), F32)
    lax.fori_loop(0, n_chunks, body, (zero, zero), unroll=HG_UNROLL)

    o = of_s[CTX_LEN:, :] + ob_s[CTX_LEN:, :]
    o = o * lax.rsqrt(jnp.mean(o * o, axis=-1, keepdims=True) + EPS)
    o_ref[...] = (o * gn_ref[0] * jax.nn.silu(og_ref[...])).astype(o_ref.dtype)


def _hgrn_branch(z, lb_logits, hg_norm, layer):
    n_slots = lb_logits.shape[1]
    ctx_blk0 = N_LAT_ROWS // CTX_LEN

    def windows(i):
        col_blk = (D_A + i * D_B) // HG_DK
        return [pl.BlockSpec((CTX_LEN, HG_DK), lambda b, h: (ctx_blk0 + b, col_blk + h)),
                pl.BlockSpec((SEQ, HG_DK), lambda b, h: (b, col_blk + h))]

    in_specs = [spec for i in range(4) for spec in windows(i)]
    in_specs += [windows(4)[1],
                 pl.BlockSpec((2, n_slots, 1, 1, HG_DK), lambda b, h: (0, 0, h, 0, 0)),
                 pl.BlockSpec((1, 1, HG_DK), lambda b, h: (h, 0, 0))]
    operand = lambda: pltpu.VMEM((2, T_ALL, HG_DK), BF16)
    lg = lb_logits.astype(F32).reshape(2, n_slots, HG_HEADS, 1, HG_DK)
    return pl.pallas_call(
        functools.partial(_hgrn_kernel, layer=layer),
        grid=(BATCH, HG_HEADS),
        in_specs=in_specs,
        out_specs=pl.BlockSpec((SEQ, HG_DK), lambda b, h: (b, h)),
        out_shape=jax.ShapeDtypeStruct((N_LAT_ROWS, D_B), BF16),
        scratch_shapes=[operand(), operand(), operand(), operand(),
                        pltpu.VMEM((T_ALL, HG_DK), BF16),
                        pltpu.VMEM((2, T_ALL // HG_CHUNK, 1, HG_DK), F32),
                        pltpu.VMEM((T_ALL, HG_DK), F32),
                        pltpu.VMEM((T_ALL, HG_DK), F32)],
        compiler_params=_params(("parallel", "parallel"), 40),
        name="hgrn2",
    )(*([z] * 9), lg, hg_norm.reshape(HG_HEADS, 1, HG_DK))


FF_DOWN_TK = D_FF // 2


def kernel(x, c, ctx, c_ctx, w_ada, b_ada, norm_ffn1, w1_ffn1, w3_ffn1, w2_ffn1, norm_mix, w_in,
           s5_lam_re, s5_lam_im, s5_log_dt, s5_b_re, s5_b_im, s5_c_re, s5_c_im, s5_d, s5_w_glu,
           hg_lb_logits, hg_norm, w_proj_a, w_proj_b, w_out, norm_ffn2, w1_ffn2, w3_ffn2, w2_ffn2,
           norm_final):
    layer = 0
    c8 = jnp.concatenate([c, c_ctx[None], jnp.zeros((8 - BATCH - 1, D_MODEL), F32)], axis=0)
    mod = _ada(c8, w_ada[layer], b_ada[layer]).reshape(8, N_MOD, D_MODEL)
    mod = jnp.concatenate([mod[BATCH:BATCH + 1], mod[:BATCH]], axis=0)
    m = [mod[:, i].reshape(BATCH + 1, 1, D_MODEL) for i in range(N_MOD)]

    x_scan = x.reshape(BATCH, GRID_ROWS, GRID_W, D_MODEL).swapaxes(1, 2).reshape(N_LAT_ROWS, D_MODEL)
    h0 = jnp.concatenate([x_scan, ctx.reshape(N_CTX_ROWS, D_MODEL)], axis=0)
    tile = 1024

    z1 = _norm_mod(h0, norm_ffn1[layer], m[0], m[1], _group_full(512))
    g1 = _swiglu_up(z1, w1_ffn1[layer], w3_ffn1[layer])
    h1 = _mm_res(g1, w2_ffn1[layer].astype(BF16), h0, 0, m[2], 0.5, _group_full(tile),
                 tile, 512, FF_DOWN_TK, 52)

    z2 = _norm_mod(h1, norm_mix[layer], m[3], m[4], _group_full(512))
    z = _mm(z2, w_in[layer], F32, tile, 512, 52, "mixer_in_proj")

    y_a = _s5_branch(z, s5_lam_re[layer], s5_lam_im[layer], s5_log_dt[layer], s5_b_re[layer],
                     s5_b_im[layer], s5_c_re[layer], s5_c_im[layer], s5_d[layer], s5_w_glu[layer])
    y_b = _hgrn_branch(z, hg_lb_logits, hg_norm[layer], layer)

    ga_tile = (D_A + 5 * D_B) // 512
    merged = _merge(y_a, w_proj_a[layer], y_b, w_proj_b[layer], z,
                    ga_tile, ga_tile + D_MODEL // 512, 0)
    h2 = _mm_res(merged, w_out[layer], h1, 0, m[5], 1.0, _group_lat(tile),
                 tile, 512, D_MODEL, 52)

    z3 = _norm_mod(h2, norm_ffn2[layer], m[6], m[7], _group_lat(512))
    g2 = _swiglu_up(z3, w1_ffn2[layer], w3_ffn2[layer])
    h3 = _mm_res(g2, w2_ffn2[layer].astype(BF16), h2, 0, m[8], 0.5, _group_lat(tile),
                 tile, 512, FF_DOWN_TK, 52)
    out = _final_norm(h3, norm_final).reshape(BATCH, GRID_W, GRID_ROWS, D_MODEL)
    return out.swapaxes(1, 2).reshape(BATCH, SEQ, D_MODEL)
```

```python
import functools
import math

import jax
import jax.numpy as jnp
from jax import lax
from jax.experimental import pallas as pl
from jax.experimental.pallas import tpu as pltpu

F32 = jnp.float32
BF16 = jnp.bfloat16

D_MODEL = 4096
BATCH = 4
SEQ = 2048
GRID_W = 64
GRID_ROWS = SEQ // GRID_W
CTX_LEN = 256
T_ALL = CTX_LEN + SEQ
D_FF = 11008
D_A = 1024
S5_H = 16
S5_G = D_A // S5_H
S5_P = 64
S5_L = 16
D_B = 2048
HG_DK = 128
HG_HEADS = D_B // HG_DK
HG_CHUNK = 64
N_MOD = 9
N_IN = D_A + 5 * D_B + 2 * D_MODEL
EPS = 1e-6

N_CTX_ROWS = BATCH * CTX_LEN
N_LAT_ROWS = BATCH * SEQ
N_ROWS = N_CTX_ROWS + N_LAT_ROWS
MIB = 1024 * 1024


def _params(semantics, vmem_mib):
    return pltpu.CompilerParams(dimension_semantics=semantics, vmem_limit_bytes=vmem_mib * MIB)


def _group_full(tm):
    n_lat_tiles, per_batch = N_LAT_ROWS // tm, SEQ // tm
    return lambda i: jnp.where(i < n_lat_tiles, 1 + i // per_batch, 0)


def _group_lat(tm):
    per_batch = SEQ // tm
    return lambda i: 1 + i // per_batch


def _ada_kernel(c_ref, w_ref, b_ref, o_ref):
    sc = jax.nn.silu(c_ref[...]).astype(BF16)
    o_ref[...] = jnp.dot(sc, w_ref[...].astype(BF16), preferred_element_type=F32) + b_ref[...]


def _ada(c8, w_ada, b_ada):
    n = w_ada.shape[1]
    tn = 512
    return pl.pallas_call(
        _ada_kernel,
        grid=(n // tn,),
        in_specs=[pl.BlockSpec((8, D_MODEL), lambda j: (0, 0)),
                  pl.BlockSpec((D_MODEL, tn), lambda j: (0, j)),
                  pl.BlockSpec((1, tn), lambda j: (0, j))],
        out_specs=pl.BlockSpec((8, tn), lambda j: (0, j)),
        out_shape=jax.ShapeDtypeStruct((8, n), F32),
        compiler_params=_params(("parallel",), 40),
        name="ada_table",
    )(c8, w_ada, b_ada.reshape(1, n))


def _norm_mod_kernel(x_ref, g_ref, sh_ref, sc_ref, o_ref):
    x = x_ref[...]
    var = jnp.mean(x * x, axis=-1, keepdims=True)
    y = x * lax.rsqrt(var + EPS) * g_ref[...]
    o_ref[...] = (y * (1.0 + sc_ref[0]) + sh_ref[0]).astype(o_ref.dtype)


def _norm_mod(x, gain, shift, scale, group_fn, tm=512):
    rows = x.shape[0]
    vec = pl.BlockSpec((1, 1, D_MODEL), lambda i: (group_fn(i), 0, 0))
    return pl.pallas_call(
        _norm_mod_kernel,
        grid=(rows // tm,),
        in_specs=[pl.BlockSpec((tm, D_MODEL), lambda i: (i, 0)),
                  pl.BlockSpec((1, D_MODEL), lambda i: (0, 0)),
                  vec, vec],
        out_specs=pl.BlockSpec((tm, D_MODEL), lambda i: (i, 0)),
        out_shape=jax.ShapeDtypeStruct((rows, D_MODEL), BF16),
        compiler_params=_params(("parallel",), 40),
        name="norm_modulate",
    )(x, gain.reshape(1, D_MODEL), shift, scale)


def _final_norm_kernel(x_ref, g_ref, o_ref):
    x = x_ref[...]
    var = jnp.mean(x * x, axis=-1, keepdims=True)
    o_ref[...] = x * lax.rsqrt(var + EPS) * g_ref[...]


def _final_norm(x, gain, tm=512):
    rows = x.shape[0]
    return pl.pallas_call(
        _final_norm_kernel,
        grid=(rows // tm,),
        in_specs=[pl.BlockSpec((tm, D_MODEL), lambda i: (i, 0)),
                  pl.BlockSpec((1, D_MODEL), lambda i: (0, 0))],
        out_specs=pl.BlockSpec((tm, D_MODEL), lambda i: (i, 0)),
        out_shape=jax.ShapeDtypeStruct((rows, D_MODEL), F32),
        compiler_params=_params(("parallel",), 40),
        name="final_norm",
    )(x, gain.reshape(1, D_MODEL))


def _swiglu_up_kernel(z_ref, w1_ref, w3_ref, o_ref):
    z = z_ref[...]
    h1 = jnp.dot(z, w1_ref[...].astype(BF16), preferred_element_type=F32)
    h3 = jnp.dot(z, w3_ref[...].astype(BF16), preferred_element_type=F32)
    o_ref[...] = (jax.nn.silu(h1) * h3).astype(o_ref.dtype)


def _swiglu_up(z, w1, w3, tm=1024, tn=512):
    m, k = z.shape
    n = w1.shape[1]
    return pl.pallas_call(
        _swiglu_up_kernel,
        grid=(m // tm, pl.cdiv(n, tn)),
        in_specs=[pl.BlockSpec((tm, k), lambda i, j: (i, 0), pipeline_mode=pl.Buffered(1)),
                  pl.BlockSpec((k, tn), lambda i, j: (0, j)),
                  pl.BlockSpec((k, tn), lambda i, j: (0, j))],
        out_specs=pl.BlockSpec((tm, tn), lambda i, j: (i, j)),
        out_shape=jax.ShapeDtypeStruct((m, n), BF16),
        compiler_params=_params(("parallel", "arbitrary"), 56),
        name="swiglu_up",
    )(z, w1, w3)


def _mm_res_kernel(a_ref, w_ref, res_ref, gate_ref, o_ref, *scratch, scale, nk):
    if nk == 1:
        prod = jnp.dot(a_ref[...], w_ref[...].astype(BF16), preferred_element_type=F32)
        o_ref[...] = res_ref[...] + (scale * gate_ref[0]) * prod
        return
    acc_ref, = scratch
    k = pl.program_id(2)

    @pl.when(k == 0)
    def _():
        acc_ref[...] = jnp.zeros_like(acc_ref)

    acc_ref[...] += jnp.dot(a_ref[...], w_ref[...].astype(BF16), preferred_element_type=F32)

    @pl.when(k == nk - 1)
    def _():
        o_ref[...] = res_ref[...] + (scale * gate_ref[0]) * acc_ref[...]


def _mm_res(a, w, res, res_tile_off, gate, scale, group_fn, tm, tn, tk, vmem_mib):
    m, k = a.shape
    n = w.shape[1]
    nk = k // tk
    return pl.pallas_call(
        functools.partial(_mm_res_kernel, scale=scale, nk=nk),
        grid=(m // tm, n // tn, nk),
        in_specs=[pl.BlockSpec((tm, tk), lambda i, j, kk: (i, kk)),
                  pl.BlockSpec((tk, tn), lambda i, j, kk: (kk, j)),
                  pl.BlockSpec((tm, tn), lambda i, j, kk: (i + res_tile_off, j)),
                  pl.BlockSpec((1, 1, tn), lambda i, j, kk: (group_fn(i), 0, j))],
        out_specs=pl.BlockSpec((tm, tn), lambda i, j, kk: (i, j)),
        out_shape=jax.ShapeDtypeStruct((m, n), F32),
        scratch_shapes=[] if nk == 1 else [pltpu.VMEM((tm, tn), F32)],
        compiler_params=_params(("parallel", "parallel", "arbitrary"), vmem_mib),
        name="matmul_gated_residual",
    )(a, w, res, gate)


def _mm_kernel(a_ref, w_ref, o_ref):
    o_ref[...] = jnp.dot(a_ref[...], w_ref[...].astype(BF16), preferred_element_type=F32).astype(o_ref.dtype)


def _mm(a, w, out_dtype, tm, tn, vmem_mib, name):
    m, k = a.shape
    n = w.shape[1]
    return pl.pallas_call(
        _mm_kernel,
        grid=(m // tm, n // tn),
        in_specs=[pl.BlockSpec((tm, k), lambda i, j: (i, 0)),
                  pl.BlockSpec((k, tn), lambda i, j: (0, j))],
        out_specs=pl.BlockSpec((tm, tn), lambda i, j: (i, j)),
        out_shape=jax.ShapeDtypeStruct((m, n), out_dtype),
        compiler_params=_params(("parallel", "arbitrary"), vmem_mib),
        name=name,
    )(a, w)


def _merge_kernel(ya_ref, wa_ref, yb_ref, wb_ref, ga_ref, gb_ref, o_ref):
    pa = jnp.dot(ya_ref[...], wa_ref[...].astype(BF16), preferred_element_type=F32)
    pb = jnp.dot(yb_ref[...], wb_ref[...].astype(BF16), preferred_element_type=F32)
    o_ref[...] = (jax.nn.sigmoid(ga_ref[...]) * pa + jax.nn.sigmoid(gb_ref[...]) * pb).astype(o_ref.dtype)


def _merge(ya, wa, yb, wb, z, ga_col_tile, gb_col_tile, z_row_tile_off, tm=1024, tn=512):
    m = ya.shape[0]
    n = wa.shape[1]
    return pl.pallas_call(
        _merge_kernel,
        grid=(m // tm, n // tn),
        in_specs=[pl.BlockSpec((tm, D_A), lambda i, j: (i, 0)),
                  pl.BlockSpec((D_A, tn), lambda i, j: (0, j)),
                  pl.BlockSpec((tm, D_B), lambda i, j: (i, 0)),
                  pl.BlockSpec((D_B, tn), lambda i, j: (0, j)),
                  pl.BlockSpec((tm, tn), lambda i, j: (i + z_row_tile_off, ga_col_tile + j)),
                  pl.BlockSpec((tm, tn), lambda i, j: (i + z_row_tile_off, gb_col_tile + j))],
        out_specs=pl.BlockSpec((tm, tn), lambda i, j: (i, j)),
        out_shape=jax.ShapeDtypeStruct((m, n), BF16),
        compiler_params=_params(("parallel", "arbitrary"), 48),
        name="gated_merge",
    )(ya, wa, yb, wb, z, z)


S5_PRECISION = lax.Precision.HIGHEST


def _cmul(a, b):
    return a[0] * b[0] - a[1] * b[1], a[0] * b[1] + a[1] * b[0]


def _s5_prep_kernel(ldt_ref, lam_row_ref, lam_col_ref, b1_ref, b2_ref, c1_ref, c2_ref, dt_ref,
                    wst_ref, wc_ref, a16_ref):
    g = pl.program_id(0)
    lh = S5_L * S5_H
    lane_sign = jnp.where(lax.broadcasted_iota(jnp.int32, (1, 2 * S5_P), 1) < S5_P, -1.0, 1.0)
    row_sign = jnp.where(lax.broadcasted_iota(jnp.int32, (2 * S5_P, 1), 0) < S5_P, 1.0, -1.0)
    log2_h = S5_H.bit_length() - 1
    log2_l = S5_L.bit_length() - 1
    t_idx = lax.shift_right_logical(lax.broadcasted_iota(jnp.int32, (1, lh), 1), log2_h)
    b1 = b1_ref[0]
    b2 = b2_ref[0]
    c1 = c1_ref[0]
    c2 = c2_ref[0]
    toep = None
    for d in range(2):
        dt = jnp.exp(jnp.full((1, 1), ldt_ref[d, g], F32))
        lr = lam_row_ref[d, 0]
        li = lam_row_ref[2 + d, 0]
        mag = jnp.exp(lr * dt)
        ab_re, ab_im = mag * jnp.cos(li * dt), mag * jnp.sin(li * dt)
        den = lr * lr + li * li
        nr = ab_re - 1.0
        kr = (nr * lr + ab_im * li) / den
        ki = (ab_im * lr - nr * li) / den
        bk = kr * b1 + lane_sign * ki * b2
        bk_sw = lane_sign * (kr * b2 - lane_sign * ki * b1)
        pw = [(jnp.ones_like(ab_re), jnp.zeros_like(ab_re))]
        for _ in range(S5_L):
            pw.append(_cmul(pw[-1], (ab_re, ab_im)))
        taus = range(S5_L - 1, -1, -1) if d == 0 else range(S5_L)
        rows_of = lambda part: jnp.concatenate(
            [jnp.broadcast_to(pw[tau][part], (S5_H, 2 * S5_P)) for tau in taus], axis=0)
        wst_ref[0, :, d * 2 * S5_P:(d + 1) * 2 * S5_P] = rows_of(0) * bk + rows_of(1) * bk_sw
        a16_ref[0, 2 * d:2 * d + 1, :] = pw[S5_L][0]
        a16_ref[0, 2 * d + 1:2 * d + 2, :] = lane_sign * pw[S5_L][1]
        lrc = lam_col_ref[d, 0]
        lic = lam_col_ref[2 + d, 0]
        magc = jnp.exp(lrc * dt)
        squares = [(magc * jnp.cos(lic * dt), magc * jnp.sin(lic * dt))]
        for _ in range(log2_l - 1):
            squares.append(_cmul(squares[-1], squares[-1]))
        tau_col = t_idx if d == 0 else (S5_L - 1) - t_idx
        q0 = None
        for j, sq in enumerate(squares):
            bit = (lax.shift_right_logical(tau_col, j) & 1) == 1
            factor = (jnp.where(bit, sq[0], 1.0), jnp.where(bit, sq[1], 0.0))
            q0 = factor if q0 is None else _cmul(q0, factor)
        q1 = _cmul(q0, squares[0])

        def c_times(q):
            return row_sign * c1 * q[0] - c2 * q[1]

        wc_ref[0, d * 2 * S5_P:(d + 1) * 2 * S5_P, :] = c_times(q1)
        m_all = jnp.dot(bk[0:S5_H], c_times(q0), precision=S5_PRECISION,
                        preferred_element_type=F32)
        lane = lax.broadcasted_iota(jnp.int32, (S5_H, lh), 1)
        blocks = []
        for s in range(S5_L):
            if d == 0:
                shift, keep = S5_H * s, lane >= S5_H * s
            else:
                shift, keep = (S5_H * (s + 1)) % lh, lane < S5_H * (s + 1)
            rolled = m_all if shift == 0 else pltpu.roll(m_all, shift, axis=1)
            blocks.append(jnp.where(keep, rolled, 0.0))
        td = jnp.concatenate(blocks, axis=0)
        toep = td if toep is None else toep + td
    diag = lax.broadcasted_iota(jnp.int32, (lh, lh), 0) == lax.broadcasted_iota(jnp.int32, (lh, lh), 1)
    wc_ref[0, 4 * S5_P:, :] = toep + jnp.where(diag, dt_ref[0], 0.0)


def _s5_prep(lam_re, lam_im, log_dt, b_re, b_im, c_re, c_im, d_skip):
    lh = S5_L * S5_H
    dup = lambda z: jnp.concatenate([z, z], axis=-1)
    lam = jnp.concatenate([lam_re, lam_im], axis=0)
    lam_row = dup(lam)[:, :, None, :]
    lam_col = dup(lam)[:, :, :, None]
    bt_re, bt_im = jnp.swapaxes(b_re, 1, 2), jnp.swapaxes(b_im, 1, 2)
    b1 = jnp.tile(jnp.concatenate([bt_re, bt_im], axis=-1), (1, S5_L, 1))
    b2 = jnp.tile(jnp.concatenate([bt_im, bt_re], axis=-1), (1, S5_L, 1))
    ct_re, ct_im = jnp.swapaxes(c_re, 1, 2), jnp.swapaxes(c_im, 1, 2)
    c1 = jnp.tile(jnp.concatenate([ct_re, ct_im], axis=1), (1, 1, S5_L))
    c2 = jnp.tile(jnp.concatenate([ct_im, ct_re], axis=1), (1, 1, S5_L))
    d_tiled = jnp.tile(d_skip.reshape(S5_G, 1, S5_H), (1, 1, S5_L))
    grp = lambda shape: pl.BlockSpec((1,) + shape, lambda g: (g, 0, 0))
    return pl.pallas_call(
        _s5_prep_kernel,
        grid=(S5_G,),
        in_specs=[pl.BlockSpec(memory_space=pltpu.SMEM),
                  pl.BlockSpec((4, 1, 1, 2 * S5_P), lambda g: (0, g, 0, 0)),
                  pl.BlockSpec((4, 1, 2 * S5_P, 1), lambda g: (0, g, 0, 0)),
                  grp((lh, 2 * S5_P)), grp((lh, 2 * S5_P)),
                  grp((2 * S5_P, lh)), grp((2 * S5_P, lh)),
                  grp((1, lh))],
        out_specs=[grp((lh, 4 * S5_P)), grp((4 * S5_P + lh, lh)), grp((4, 2 * S5_P))],
        out_shape=[jax.ShapeDtypeStruct((S5_G, lh, 4 * S5_P), F32),
                   jax.ShapeDtypeStruct((S5_G, 4 * S5_P + lh, lh), F32),
                   jax.ShapeDtypeStruct((S5_G, 4, 2 * S5_P), F32)],
        compiler_params=_params(("parallel",), 32),
        name="s5_prep",
    )(log_dt, lam_row, lam_col, b1, b2, c1, c2, d_tiled)


def _s5_local_kernel(u_ref, wst_ref, xf_ref, xb_ref):
    r = jnp.dot(u_ref[0], wst_ref[0], precision=S5_PRECISION, preferred_element_type=F32)
    xf_ref[...] = r[:, :2 * S5_P]
    xb_ref[...] = r[:, 2 * S5_P:]


def _s5_local(ug, wst):
    rows = ug.shape[1]
    lh = S5_L * S5_H
    col = pl.BlockSpec((rows, 2 * S5_P), lambda g: (0, g))
    shape = jax.ShapeDtypeStruct((rows, S5_G * 2 * S5_P), F32)
    return pl.pallas_call(
        _s5_local_kernel,
        grid=(S5_G,),
        in_specs=[pl.BlockSpec((1, rows, lh), lambda g: (g, 0, 0)),
                  pl.BlockSpec((1, lh, 4 * S5_P), lambda g: (g, 0, 0))],
        out_specs=[col, col],
        out_shape=[shape, shape],
        compiler_params=_params(("parallel",), 32),
        name="s5_local_state",
    )(ug, wst)


S5_SCAN_BLOCK = 8


def _s5_scan_kernel(xf_ref, xb_ref, a_ref, pf_ref, pb_ref, sf_ref, sb_ref):
    @pl.when(pl.program_id(0) == 0)
    def _():
        sf_ref[...] = jnp.zeros_like(sf_ref)
        sb_ref[...] = jnp.zeros_like(sb_ref)

    def step(s, x, aa, ab):
        return aa * s + ab * pltpu.roll(s, S5_P, axis=1) + x

    s = sf_ref[...]
    for j in range(S5_SCAN_BLOCK):
        pf_ref[j] = s
        s = step(s, xf_ref[j], a_ref[0], a_ref[1])
    sf_ref[...] = s
    s = sb_ref[...]
    for j in reversed(range(S5_SCAN_BLOCK)):
        pb_ref[j] = s
        s = step(s, xb_ref[j], a_ref[2], a_ref[3])
    sb_ref[...] = s


def _s5_scan(xf, xb, a16):
    n_chunks, rows, lanes = xf.shape
    nblk = n_chunks // S5_SCAN_BLOCK
    ctx_blk = (CTX_LEN // S5_L) // S5_SCAN_BLOCK
    fwd = lambda i: (i, 0, 0)
    bwd = lambda i: (jnp.where(i < ctx_blk, ctx_blk - 1 - i, nblk - 1 + ctx_blk - i), 0, 0)
    blk = (S5_SCAN_BLOCK, rows, lanes)
    shape = jax.ShapeDtypeStruct(xf.shape, F32)
    return pl.pallas_call(
        _s5_scan_kernel,
        grid=(nblk,),
        in_specs=[pl.BlockSpec(blk, fwd), pl.BlockSpec(blk, bwd),
                  pl.BlockSpec((4, rows, lanes), lambda i: (0, 0, 0))],
        out_specs=[pl.BlockSpec(blk, fwd), pl.BlockSpec(blk, bwd)],
        out_shape=[shape, shape],
        scratch_shapes=[pltpu.VMEM((rows, lanes), F32), pltpu.VMEM((rows, lanes), F32)],
        compiler_params=_params(("arbitrary",), 32),
        name="s5_chunk_scan",
    )(xf, xb, a16)


def _s5_out_kernel(pf_ref, pb_ref, u_ref, wc_ref, y_ref):
    dot = functools.partial(jnp.dot, precision=S5_PRECISION, preferred_element_type=F32)
    y = (dot(pf_ref[...], wc_ref[0, 0:2 * S5_P, :])
         + dot(pb_ref[...], wc_ref[0, 2 * S5_P:4 * S5_P, :])
         + dot(u_ref[0], wc_ref[0, 4 * S5_P:, :]))
    y_ref[0] = jax.nn.gelu(y)


def _s5_out(pf, pb, ug, wc):
    rows = ug.shape[1]
    lh = S5_L * S5_H
    col = pl.BlockSpec((rows, 2 * S5_P), lambda g: (0, g))
    return pl.pallas_call(
        _s5_out_kernel,
        grid=(S5_G,),
        in_specs=[col, col,
                  pl.BlockSpec((1, rows, lh), lambda g: (g, 0, 0)),
                  pl.BlockSpec((1, 4 * S5_P + lh, lh), lambda g: (g, 0, 0))],
        out_specs=pl.BlockSpec((1, rows, lh), lambda g: (g, 0, 0)),
        out_shape=jax.ShapeDtypeStruct((S5_G, rows, lh), F32),
        compiler_params=_params(("parallel",), 32),
        name="s5_output",
    )(pf, pb, ug, wc)


def _s5_glu_kernel(y_ref, w_ref, o_ref):
    y = y_ref[...]
    gate = jnp.dot(y.astype(BF16), w_ref[...].astype(BF16), preferred_element_type=F32)
    o_ref[...] = (y * jax.nn.sigmoid(gate)).astype(o_ref.dtype)


def _s5_glu(y, w_glu, tm=1024):
    m = y.shape[0]
    return pl.pallas_call(
        _s5_glu_kernel,
        grid=(m // tm,),
        in_specs=[pl.BlockSpec((tm, D_A), lambda i: (i, 0)),
                  pl.BlockSpec((D_A, D_A), lambda i: (0, 0))],
        out_specs=pl.BlockSpec((tm, D_A), lambda i: (i, 0)),
        out_shape=jax.ShapeDtypeStruct((m, D_A), BF16),
        compiler_params=_params(("parallel",), 32),
        name="s5_glu",
    )(y, w_glu)


def _s5_branch(z, lam_re, lam_im, log_dt, b_re, b_im, c_re, c_im, d_skip, w_glu):
    wst, wc, a16 = _s5_prep(lam_re, lam_im, log_dt, b_re, b_im, c_re, c_im, d_skip)
    nc_ctx, nc_lat = CTX_LEN // S5_L, SEQ // S5_L
    n_chunks = nc_ctx + nc_lat
    u = z[:, :D_A]
    u_ctx = u[N_LAT_ROWS:].reshape(BATCH, nc_ctx, S5_L, S5_G, S5_H).transpose(3, 1, 0, 2, 4)
    u_lat = u[:N_LAT_ROWS].reshape(BATCH, GRID_W // S5_L, S5_L, GRID_ROWS, S5_G, S5_H)
    u_lat = u_lat.transpose(4, 3, 1, 0, 2, 5).reshape(S5_G, nc_lat, BATCH, S5_L, S5_H)
    ug = jnp.concatenate([u_ctx, u_lat], axis=1).reshape(S5_G, n_chunks * BATCH, S5_L * S5_H)
    xf, xb = _s5_local(ug, wst)
    as_scan = lambda x: x.reshape(n_chunks, BATCH * S5_G, 2 * S5_P)
    a16_rows = jnp.tile(jnp.swapaxes(a16, 0, 1), (1, BATCH, 1))
    pf, pb = _s5_scan(as_scan(xf), as_scan(xb), a16_rows)
    as_cols = lambda x: x.reshape(n_chunks * BATCH, S5_G * 2 * S5_P)
    yg = _s5_out(as_cols(pf), as_cols(pb), ug, wc)
    y_lat = yg.reshape(S5_G, n_chunks, BATCH, S5_L, S5_H)[:, nc_ctx:]
    y_lat = y_lat.reshape(S5_G, GRID_ROWS, GRID_W // S5_L, BATCH, S5_L, S5_H)
    y_lat = y_lat.transpose(3, 2, 4, 1, 0, 5).reshape(N_LAT_ROWS, D_A)
    return _s5_glu(y_lat, w_glu)


HG_PREP_ROWS = 256
HG_UNROLL = 6


def _chunk_cumsum(x, reverse):
    n = x.shape[0]
    pos = lax.broadcasted_iota(jnp.int32, (n, 1), 0) & (HG_CHUNK - 1)
    s = 1
    while s < HG_CHUNK:
        if reverse:
            x = x + jnp.where(pos < HG_CHUNK - s, pltpu.roll(x, n - s, axis=0), 0.0)
        else:
            x = x + jnp.where(pos >= s, pltpu.roll(x, s, axis=0), 0.0)
        s *= 2
    return x


def _hgrn_kernel(qc_ref, ql_ref, vc_ref, vl_ref, ffc_ref, ffl_ref, fbc_ref, fbl_ref, og_ref, lg_ref,
                 gn_ref, o_ref, qe_s, ke_s, qb_s, kd_s, v_s, dec_s, of_s, ob_s, *, layer):
    c = HG_CHUNK
    n_chunks = T_ALL // c
    n_ctx_chunks = CTX_LEN // c
    row = lax.broadcasted_iota(jnp.int32, (c, c), 0)
    col = lax.broadcasted_iota(jnp.int32, (c, c), 1)
    causal = row >= col
    anti = row <= col
    nt = (((1,), (1,)), ((), ()))
    tn = (((0,), (0,)), ((), ()))

    def lower_bound(d):
        logits = [lg_ref[d, j, 0] for j in range(lg_ref.shape[1])]
        top = functools.reduce(jnp.maximum, logits)
        e = [jnp.exp(l - top) for l in logits]
        return sum(e[:layer + 1]) / sum(e)

    lb = (lower_bound(0), lower_bound(1))

    def prepare(row0, chunk0, q, v, fpre):
        n = q.shape[0]
        rows = pl.ds(row0, n)
        as_chunks = lambda a: a.reshape(n // c, c, HG_DK)
        flat_bf16 = lambda a: a.reshape(n, HG_DK).astype(BF16)
        v_s[rows, :] = v.astype(BF16)
        q3 = as_chunks(q)
        for d in range(2):
            f = lb[d] + (1.0 - lb[d]) * jax.nn.sigmoid(fpre[d])
            k3 = as_chunks(1.0 - f)
            b3 = as_chunks(_chunk_cumsum(jnp.log(f), reverse=d == 1))
            mid = c // 2 - 1 if d == 0 else c // 2
            last = c - 1 if d == 0 else 0
            b_mid, b_last = b3[:, mid:mid + 1, :], b3[:, last:last + 1, :]
            qe_s[d, rows, :] = flat_bf16(q3 * jnp.exp(b3 - b_mid))
            ke_s[d, rows, :] = flat_bf16(k3 * jnp.exp(b_mid - b3))
            qb_s[d, rows, :] = flat_bf16(q3 * jnp.exp(b3))
            kd_s[d, rows, :] = flat_bf16(k3 * jnp.exp(b_last - b3))
            dec_s[d, pl.ds(chunk0, n // c)] = jnp.exp(b_last)

    prepare(0, 0, qc_ref[...], vc_ref[...], (ffc_ref[...], fbc_ref[...]))

    def prep_body(blk, carry):
        r0 = pl.multiple_of(blk * HG_PREP_ROWS, HG_PREP_ROWS)
        src = pl.ds(r0, HG_PREP_ROWS)
        prepare(pl.multiple_of(CTX_LEN + r0, HG_PREP_ROWS), n_ctx_chunks + blk * (HG_PREP_ROWS // c),
                ql_ref[src, :], vl_ref[src, :], (ffl_ref[src, :], fbl_ref[src, :]))
        return carry

    lax.fori_loop(0, SEQ // HG_PREP_ROWS, prep_body, 0)

    def chunk(cidx, d, state_t):
        rows = pl.ds(pl.multiple_of(cidx * c, c), c)
        v = v_s[rows, :]
        scores = lax.dot_general(qe_s[d, rows, :], ke_s[d, rows, :], nt, preferred_element_type=F32)
        scores = jnp.where(causal if d == 0 else anti, scores, 0.0)
        o = jnp.dot(scores.astype(BF16), v, preferred_element_type=F32)
        o = o + lax.dot_general(qb_s[d, rows, :], state_t.astype(BF16), nt, preferred_element_type=F32)
        kv_t = lax.dot_general(v, kd_s[d, rows, :], tn, preferred_element_type=F32)
        (of_s if d == 0 else ob_s)[rows, :] = o
        return state_t * dec_s[d, cidx] + kv_t

    def body(i, carry):
        sf, sb = carry
        sf = chunk(i, 0, sf)
        jb = jnp.where(i < n_ctx_chunks, n_ctx_chunks - 1 - i, n_chunks - 1 + n_ctx_chunks - i)
        sb = chunk(jb, 1, sb)
        return sf, sb

    zero = jnp.zeros((HG_DK, HG_DK), F32)
    lax.fori_loop(0, n_chunks, body, (zero, zero), unroll=HG_UNROLL)

    o = of_s[CTX_LEN:, :] + ob_s[CTX_LEN:, :]
    o = o * lax.rsqrt(jnp.mean(o * o, axis=-1, keepdims=True) + EPS)
    o_ref[...] = (o * gn_ref[0] * jax.nn.silu(og_ref[...])).astype(o_ref.dtype)


def _hgrn_branch(z, lb_logits, hg_norm, layer):
    n_slots = lb_logits.shape[1]
    ctx_blk0 = N_LAT_ROWS // CTX_LEN

    def windows(i):
        col_blk = (D_A + i * D_B) // HG_DK
        return [pl.BlockSpec((CTX_LEN, HG_DK), lambda b, h: (ctx_blk0 + b, col_blk + h)),
                pl.BlockSpec((SEQ, HG_DK), lambda b, h: (b, col_blk + h))]

    in_specs = [spec for i in range(4) for spec in windows(i)]
    in_specs += [windows(4)[1],
                 pl.BlockSpec((2, n_slots, 1, 1, HG_DK), lambda b, h: (0, 0, h, 0, 0)),
                 pl.BlockSpec((1, 1, HG_DK), lambda b, h: (h, 0, 0))]
    operand = lambda: pltpu.VMEM((2, T_ALL, HG_DK), BF16)
    lg = lb_logits.astype(F32).reshape(2, n_slots, HG_HEADS, 1, HG_DK)
    return pl.pallas_call(
        functools.partial(_hgrn_kernel, layer=layer),
        grid=(BATCH, HG_HEADS),
        in_specs=in_specs,
        out_specs=pl.BlockSpec((SEQ, HG_DK), lambda b, h: (b, h)),
        out_shape=jax.ShapeDtypeStruct((N_LAT_ROWS, D_B), BF16),
        scratch_shapes=[operand(), operand(), operand(), operand(),
                        pltpu.VMEM((T_ALL, HG_DK), BF16),
                        pltpu.VMEM((2, T_ALL // HG_CHUNK, 1, HG_DK), F32),
                        pltpu.VMEM((T_ALL, HG_DK), F32),
                        pltpu.VMEM((T_ALL, HG_DK), F32)],
        compiler_params=_params(("parallel", "parallel"), 40),
        name="hgrn2",
    )(*([z] * 9), lg, hg_norm.reshape(HG_HEADS, 1, HG_DK))


FF_DOWN_TILE = 512


def kernel(x, c, ctx, c_ctx, w_ada, b_ada, norm_ffn1, w1_ffn1, w3_ffn1, w2_ffn1, norm_mix, w_in,
           s5_lam_re, s5_lam_im, s5_log_dt, s5_b_re, s5_b_im, s5_c_re, s5_c_im, s5_d, s5_w_glu,
           hg_lb_logits, hg_norm, w_proj_a, w_proj_b, w_out, norm_ffn2, w1_ffn2, w3_ffn2, w2_ffn2,
           norm_final):
    layer = 0
    c8 = jnp.concatenate([c, c_ctx[None], jnp.zeros((8 - BATCH - 1, D_MODEL), F32)], axis=0)
    mod = _ada(c8, w_ada[layer], b_ada[layer]).reshape(8, N_MOD, D_MODEL)
    mod = jnp.concatenate([mod[BATCH:BATCH + 1], mod[:BATCH]], axis=0)
    m = [mod[:, i].reshape(BATCH + 1, 1, D_MODEL) for i in range(N_MOD)]

    x_scan = x.reshape(BATCH, GRID_ROWS, GRID_W, D_MODEL).swapaxes(1, 2).reshape(N_LAT_ROWS, D_MODEL)
    h0 = jnp.concatenate([x_scan, ctx.reshape(N_CTX_ROWS, D_MODEL)], axis=0)
    tile = 1024

    z1 = _norm_mod(h0, norm_ffn1[layer], m[0], m[1], _group_full(512))
    g1 = _swiglu_up(z1, w1_ffn1[layer], w3_ffn1[layer])
    h1 = _mm_res(g1, w2_ffn1[layer].astype(BF16), h0, 0, m[2], 0.5, _group_full(FF_DOWN_TILE),
                 FF_DOWN_TILE, FF_DOWN_TILE, D_FF, 56)

    z2 = _norm_mod(h1, norm_mix[layer], m[3], m[4], _group_full(512))
    z = _mm(z2, w_in[layer], F32, tile, 512, 52, "mixer_in_proj")

    y_a = _s5_branch(z, s5_lam_re[layer], s5_lam_im[layer], s5_log_dt[layer], s5_b_re[layer],
                     s5_b_im[layer], s5_c_re[layer], s5_c_im[layer], s5_d[layer], s5_w_glu[layer])
    y_b = _hgrn_branch(z, hg_lb_logits, hg_norm[layer], layer)

    ga_tile = (D_A + 5 * D_B) // 512
    merged = _merge(y_a, w_proj_a[layer], y_b, w_proj_b[layer], z,
                    ga_tile, ga_tile + D_MODEL // 512, 0)
    h2 = _mm_res(merged, w_out[layer], h1, 0, m[5], 1.0, _group_lat(tile),
                 tile, 512, D_MODEL, 52)

    z3 = _norm_mod(h2, norm_ffn2[layer], m[6], m[7], _group_lat(512))
    g2 = _swiglu_up(z3, w1_ffn2[layer], w3_ffn2[layer])
    h3 = _mm_res(g2, w2_ffn2[layer].astype(BF16), h2, 0, m[8], 0.5, _group_lat(FF_DOWN_TILE),
                 FF_DOWN_TILE, FF_DOWN_TILE, D_FF, 56)
    out = _final_norm(h3, norm_final).reshape(BATCH, GRID_W, GRID_ROWS, D_MODEL)
    return out.swapaxes(1, 2).reshape(BATCH, SEQ, D_MODEL)
```

```python
import functools
import math

import jax
import jax.numpy as jnp
from jax import lax
from jax.experimental import pallas as pl
from jax.experimental.pallas import tpu as pltpu

F32 = jnp.float32
BF16 = jnp.bfloat16

D_MODEL = 4096
BATCH = 4
SEQ = 2048
GRID_W = 64
GRID_ROWS = SEQ // GRID_W
CTX_LEN = 256
T_ALL = CTX_LEN + SEQ
D_FF = 11008
D_A = 1024
S5_H = 16
S5_G = D_A // S5_H
S5_P = 64
S5_L = 16
D_B = 2048
HG_DK = 128
HG_HEADS = D_B // HG_DK
HG_CHUNK = 64
N_MOD = 9
N_IN = D_A + 5 * D_B + 2 * D_MODEL
EPS = 1e-6

N_CTX_ROWS = BATCH * CTX_LEN
N_LAT_ROWS = BATCH * SEQ
N_ROWS = N_CTX_ROWS + N_LAT_ROWS
MIB = 1024 * 1024


def _params(semantics, vmem_mib):
    return pltpu.CompilerParams(dimension_semantics=semantics, vmem_limit_bytes=vmem_mib * MIB)


def _group_full(tm):
    n_lat_tiles, per_batch = N_LAT_ROWS // tm, SEQ // tm
    return lambda i: jnp.where(i < n_lat_tiles, 1 + i // per_batch, 0)


def _group_lat(tm):
    per_batch = SEQ // tm
    return lambda i: 1 + i // per_batch


def _ada_kernel(c_ref, w_ref, b_ref, o_ref):
    sc = jax.nn.silu(c_ref[...]).astype(BF16)
    o_ref[...] = jnp.dot(sc, w_ref[...].astype(BF16), preferred_element_type=F32) + b_ref[...]


def _ada(c8, w_ada, b_ada):
    n = w_ada.shape[1]
    tn = 1024
    return pl.pallas_call(
        _ada_kernel,
        grid=(n // tn,),
        in_specs=[pl.BlockSpec((8, D_MODEL), lambda j: (0, 0)),
                  pl.BlockSpec((D_MODEL, tn), lambda j: (0, j)),
                  pl.BlockSpec((1, tn), lambda j: (0, j))],
        out_specs=pl.BlockSpec((8, tn), lambda j: (0, j)),
        out_shape=jax.ShapeDtypeStruct((8, n), F32),
        compiler_params=_params(("parallel",), 48),
        name="ada_table",
    )(c8, w_ada, b_ada.reshape(1, n))


def _norm_mod_kernel(x_ref, g_ref, sh_ref, sc_ref, o_ref):
    x = x_ref[...]
    var = jnp.mean(x * x, axis=-1, keepdims=True)
    y = x * lax.rsqrt(var + EPS) * g_ref[...]
    o_ref[...] = (y * (1.0 + sc_ref[0]) + sh_ref[0]).astype(o_ref.dtype)


def _norm_mod(x, gain, shift, scale, group_fn, tm=512):
    rows = x.shape[0]
    vec = pl.BlockSpec((1, 1, D_MODEL), lambda i: (group_fn(i), 0, 0))
    return pl.pallas_call(
        _norm_mod_kernel,
        grid=(rows // tm,),
        in_specs=[pl.BlockSpec((tm, D_MODEL), lambda i: (i, 0)),
                  pl.BlockSpec((1, D_MODEL), lambda i: (0, 0)),
                  vec, vec],
        out_specs=pl.BlockSpec((tm, D_MODEL), lambda i: (i, 0)),
        out_shape=jax.ShapeDtypeStruct((rows, D_MODEL), BF16),
        compiler_params=_params(("parallel",), 40),
        name="norm_modulate",
    )(x, gain.reshape(1, D_MODEL), shift, scale)


def _final_norm_kernel(x_ref, g_ref, o_ref):
    x = x_ref[...]
    var = jnp.mean(x * x, axis=-1, keepdims=True)
    o_ref[...] = x * lax.rsqrt(var + EPS) * g_ref[...]


def _final_norm(x, gain, tm=512):
    rows = x.shape[0]
    return pl.pallas_call(
        _final_norm_kernel,
        grid=(rows // tm,),
        in_specs=[pl.BlockSpec((tm, D_MODEL), lambda i: (i, 0)),
                  pl.BlockSpec((1, D_MODEL), lambda i: (0, 0))],
        out_specs=pl.BlockSpec((tm, D_MODEL), lambda i: (i, 0)),
        out_shape=jax.ShapeDtypeStruct((rows, D_MODEL), F32),
        compiler_params=_params(("parallel",), 40),
        name="final_norm",
    )(x, gain.reshape(1, D_MODEL))


W2_CAST_ROWS = 64


def _swiglu_up_kernel(z_ref, w1_ref, w3_ref, w2_ref, o_ref, w2_bf16_ref):
    z = z_ref[...]
    h1 = jnp.dot(z, w1_ref[...].astype(BF16), preferred_element_type=F32)
    h3 = jnp.dot(z, w3_ref[...].astype(BF16), preferred_element_type=F32)
    o_ref[...] = (jax.nn.silu(h1) * h3).astype(o_ref.dtype)
    w2_bf16_ref[...] = w2_ref[...].astype(BF16)


def _swiglu_up(z, w1, w3, w2, tm=1024, tn=512):
    m, k = z.shape
    n = w1.shape[1]
    nj = pl.cdiv(n, tn)
    n_w2_blocks = w2.shape[0] // W2_CAST_ROWS
    assert (m // tm) * nj >= n_w2_blocks and w2.shape[0] % W2_CAST_ROWS == 0
    w2_spec = pl.BlockSpec((W2_CAST_ROWS, w2.shape[1]),
                           lambda i, j: (jnp.minimum(i * nj + j, n_w2_blocks - 1), 0))
    return pl.pallas_call(
        _swiglu_up_kernel,
        grid=(m // tm, nj),
        in_specs=[pl.BlockSpec((tm, k), lambda i, j: (i, 0), pipeline_mode=pl.Buffered(1)),
                  pl.BlockSpec((k, tn), lambda i, j: (0, j)),
                  pl.BlockSpec((k, tn), lambda i, j: (0, j)),
                  w2_spec],
        out_specs=[pl.BlockSpec((tm, tn), lambda i, j: (i, j)), w2_spec],
        out_shape=[jax.ShapeDtypeStruct((m, n), BF16), jax.ShapeDtypeStruct(w2.shape, BF16)],
        compiler_params=_params(("arbitrary", "arbitrary"), 56),
        name="swiglu_up",
    )(z, w1, w3, w2)


def _mm_res_kernel(a_ref, w_ref, res_ref, gate_ref, o_ref, *scratch, scale, nk):
    if nk == 1:
        prod = jnp.dot(a_ref[...], w_ref[...].astype(BF16), preferred_element_type=F32)
        o_ref[...] = res_ref[...] + (scale * gate_ref[0]) * prod
        return
    acc_ref, = scratch
    k = pl.program_id(2)

    @pl.when(k == 0)
    def _():
        acc_ref[...] = jnp.zeros_like(acc_ref)

    acc_ref[...] += jnp.dot(a_ref[...], w_ref[...].astype(BF16), preferred_element_type=F32)

    @pl.when(k == nk - 1)
    def _():
        o_ref[...] = res_ref[...] + (scale * gate_ref[0]) * acc_ref[...]


def _mm_res(a, w, res, res_tile_off, gate, scale, group_fn, tm, tn, tk, vmem_mib):
    m, k = a.shape
    n = w.shape[1]
    nk = k // tk
    return pl.pallas_call(
        functools.partial(_mm_res_kernel, scale=scale, nk=nk),
        grid=(m // tm, n // tn, nk),
        in_specs=[pl.BlockSpec((tm, tk), lambda i, j, kk: (i, kk)),
                  pl.BlockSpec((tk, tn), lambda i, j, kk: (kk, j)),
                  pl.BlockSpec((tm, tn), lambda i, j, kk: (i + res_tile_off, j)),
                  pl.BlockSpec((1, 1, tn), lambda i, j, kk: (group_fn(i), 0, j))],
        out_specs=pl.BlockSpec((tm, tn), lambda i, j, kk: (i, j)),
        out_shape=jax.ShapeDtypeStruct((m, n), F32),
        scratch_shapes=[] if nk == 1 else [pltpu.VMEM((tm, tn), F32)],
        compiler_params=_params(("parallel", "parallel", "arbitrary"), vmem_mib),
        name="matmul_gated_residual",
    )(a, w, res, gate)


def _mm_kernel(a_ref, w_ref, o_ref):
    o_ref[...] = jnp.dot(a_ref[...], w_ref[...].astype(BF16), preferred_element_type=F32).astype(o_ref.dtype)


def _mm(a, w, out_dtype, tm, tn, vmem_mib, name):
    m, k = a.shape
    n = w.shape[1]
    return pl.pallas_call(
        _mm_kernel,
        grid=(m // tm, n // tn),
        in_specs=[pl.BlockSpec((tm, k), lambda i, j: (i, 0)),
                  pl.BlockSpec((k, tn), lambda i, j: (0, j))],
        out_specs=pl.BlockSpec((tm, tn), lambda i, j: (i, j)),
        out_shape=jax.ShapeDtypeStruct((m, n), out_dtype),
        compiler_params=_params(("parallel", "arbitrary"), vmem_mib),
        name=name,
    )(a, w)


def _merge_kernel(ya_ref, wa_ref, yb_ref, wb_ref, ga_ref, gb_ref, o_ref):
    pa = jnp.dot(ya_ref[...], wa_ref[...].astype(BF16), preferred_element_type=F32)
    pb = jnp.dot(yb_ref[...], wb_ref[...].astype(BF16), preferred_element_type=F32)
    o_ref[...] = (jax.nn.sigmoid(ga_ref[...]) * pa + jax.nn.sigmoid(gb_ref[...]) * pb).astype(o_ref.dtype)


def _merge(ya, wa, yb, wb, z, ga_col_tile, gb_col_tile, z_row_tile_off, tm=1024, tn=512):
    m = ya.shape[0]
    n = wa.shape[1]
    return pl.pallas_call(
        _merge_kernel,
        grid=(m // tm, n // tn),
        in_specs=[pl.BlockSpec((tm, D_A), lambda i, j: (i, 0)),
                  pl.BlockSpec((D_A, tn), lambda i, j: (0, j)),
                  pl.BlockSpec((tm, D_B), lambda i, j: (i, 0)),
                  pl.BlockSpec((D_B, tn), lambda i, j: (0, j)),
                  pl.BlockSpec((tm, tn), lambda i, j: (i + z_row_tile_off, ga_col_tile + j)),
                  pl.BlockSpec((tm, tn), lambda i, j: (i + z_row_tile_off, gb_col_tile + j))],
        out_specs=pl.BlockSpec((tm, tn), lambda i, j: (i, j)),
        out_shape=jax.ShapeDtypeStruct((m, n), BF16),
        compiler_params=_params(("parallel", "arbitrary"), 48),
        name="gated_merge",
    )(ya, wa, yb, wb, z, z)


S5_PRECISION = lax.Precision.HIGHEST


def _cmul(a, b):
    return a[0] * b[0] - a[1] * b[1], a[0] * b[1] + a[1] * b[0]


def _s5_prep_kernel(ldt_ref, lam_row_ref, lam_col_ref, b1_ref, b2_ref, c1_ref, c2_ref, dt_ref,
                    wst_ref, wc_ref, a16_ref):
    g = pl.program_id(0)
    lh = S5_L * S5_H
    lane_sign = jnp.where(lax.broadcasted_iota(jnp.int32, (1, 2 * S5_P), 1) < S5_P, -1.0, 1.0)
    row_sign = jnp.where(lax.broadcasted_iota(jnp.int32, (2 * S5_P, 1), 0) < S5_P, 1.0, -1.0)
    log2_h = S5_H.bit_length() - 1
    log2_l = S5_L.bit_length() - 1
    t_idx = lax.shift_right_logical(lax.broadcasted_iota(jnp.int32, (1, lh), 1), log2_h)
    b1 = b1_ref[0]
    b2 = b2_ref[0]
    c1 = c1_ref[0]
    c2 = c2_ref[0]
    toep = None
    for d in range(2):
        dt = jnp.exp(jnp.full((1, 1), ldt_ref[d, g], F32))
        lr = lam_row_ref[d, 0]
        li = lam_row_ref[2 + d, 0]
        mag = jnp.exp(lr * dt)
        ab_re, ab_im = mag * jnp.cos(li * dt), mag * jnp.sin(li * dt)
        den = lr * lr + li * li
        nr = ab_re - 1.0
        kr = (nr * lr + ab_im * li) / den
        ki = (ab_im * lr - nr * li) / den
        bk = kr * b1 + lane_sign * ki * b2
        bk_sw = lane_sign * (kr * b2 - lane_sign * ki * b1)
        pw = [(jnp.ones_like(ab_re), jnp.zeros_like(ab_re))]
        for _ in range(S5_L):
            pw.append(_cmul(pw[-1], (ab_re, ab_im)))
        taus = range(S5_L - 1, -1, -1) if d == 0 else range(S5_L)
        rows_of = lambda part: jnp.concatenate(
            [jnp.broadcast_to(pw[tau][part], (S5_H, 2 * S5_P)) for tau in taus], axis=0)
        wst_ref[0, :, d * 2 * S5_P:(d + 1) * 2 * S5_P] = rows_of(0) * bk + rows_of(1) * bk_sw
        a16_ref[0, 2 * d:2 * d + 1, :] = pw[S5_L][0]
        a16_ref[0, 2 * d + 1:2 * d + 2, :] = lane_sign * pw[S5_L][1]
        lrc = lam_col_ref[d, 0]
        lic = lam_col_ref[2 + d, 0]
        magc = jnp.exp(lrc * dt)
        squares = [(magc * jnp.cos(lic * dt), magc * jnp.sin(lic * dt))]
        for _ in range(log2_l - 1):
            squares.append(_cmul(squares[-1], squares[-1]))
        tau_col = t_idx if d == 0 else (S5_L - 1) - t_idx
        q0 = None
        for j, sq in enumerate(squares):
            bit = (lax.shift_right_logical(tau_col, j) & 1) == 1
            factor = (jnp.where(bit, sq[0], 1.0), jnp.where(bit, sq[1], 0.0))
            q0 = factor if q0 is None else _cmul(q0, factor)
        q1 = _cmul(q0, squares[0])

        def c_times(q):
            return row_sign * c1 * q[0] - c2 * q[1]

        wc_ref[0, d * 2 * S5_P:(d + 1) * 2 * S5_P, :] = c_times(q1)
        m_all = jnp.dot(bk[0:S5_H], c_times(q0), precision=S5_PRECISION,
                        preferred_element_type=F32)
        lane = lax.broadcasted_iota(jnp.int32, (S5_H, lh), 1)
        blocks = []
        for s in range(S5_L):
            if d == 0:
                shift, keep = S5_H * s, lane >= S5_H * s
            else:
                shift, keep = (S5_H * (s + 1)) % lh, lane < S5_H * (s + 1)
            rolled = m_all if shift == 0 else pltpu.roll(m_all, shift, axis=1)
            blocks.append(jnp.where(keep, rolled, 0.0))
        td = jnp.concatenate(blocks, axis=0)
        toep = td if toep is None else toep + td
    diag = lax.broadcasted_iota(jnp.int32, (lh, lh), 0) == lax.broadcasted_iota(jnp.int32, (lh, lh), 1)
    wc_ref[0, 4 * S5_P:, :] = toep + jnp.where(diag, dt_ref[0], 0.0)


def _s5_prep(lam_re, lam_im, log_dt, b_re, b_im, c_re, c_im, d_skip):
    lh = S5_L * S5_H
    dup = lambda z: jnp.concatenate([z, z], axis=-1)
    lam = jnp.concatenate([lam_re, lam_im], axis=0)
    lam_row = dup(lam)[:, :, None, :]
    lam_col = dup(lam)[:, :, :, None]
    bt_re, bt_im = jnp.swapaxes(b_re, 1, 2), jnp.swapaxes(b_im, 1, 2)
    b1 = jnp.tile(jnp.concatenate([bt_re, bt_im], axis=-1), (1, S5_L, 1))
    b2 = jnp.tile(jnp.concatenate([bt_im, bt_re], axis=-1), (1, S5_L, 1))
    ct_re, ct_im = jnp.swapaxes(c_re, 1, 2), jnp.swapaxes(c_im, 1, 2)
    c1 = jnp.tile(jnp.concatenate([ct_re, ct_im], axis=1), (1, 1, S5_L))
    c2 = jnp.tile(jnp.concatenate([ct_im, ct_re], axis=1), (1, 1, S5_L))
    d_tiled = jnp.tile(d_skip.reshape(S5_G, 1, S5_H), (1, 1, S5_L))
    grp = lambda shape: pl.BlockSpec((1,) + shape, lambda g: (g, 0, 0))
    return pl.pallas_call(
        _s5_prep_kernel,
        grid=(S5_G,),
        in_specs=[pl.BlockSpec(memory_space=pltpu.SMEM),
                  pl.BlockSpec((4, 1, 1, 2 * S5_P), lambda g: (0, g, 0, 0)),
                  pl.BlockSpec((4, 1, 2 * S5_P, 1), lambda g: (0, g, 0, 0)),
                  grp((lh, 2 * S5_P)), grp((lh, 2 * S5_P)),
                  grp((2 * S5_P, lh)), grp((2 * S5_P, lh)),
                  grp((1, lh))],
        out_specs=[grp((lh, 4 * S5_P)), grp((4 * S5_P + lh, lh)), grp((4, 2 * S5_P))],
        out_shape=[jax.ShapeDtypeStruct((S5_G, lh, 4 * S5_P), F32),
                   jax.ShapeDtypeStruct((S5_G, 4 * S5_P + lh, lh), F32),
                   jax.ShapeDtypeStruct((S5_G, 4, 2 * S5_P), F32)],
        compiler_params=_params(("parallel",), 32),
        name="s5_prep",
    )(log_dt, lam_row, lam_col, b1, b2, c1, c2, d_tiled)


def _s5_local_kernel(ul_ref, uc_ref, wst_ref, xf_ref, xb_ref):
    n_lat = ul_ref.shape[1]
    dot = functools.partial(jnp.dot, precision=S5_PRECISION, preferred_element_type=F32)
    r_lat = dot(ul_ref[0], wst_ref[0])
    r_ctx = dot(uc_ref[0], wst_ref[0])
    xf_ref[0:n_lat, :] = r_lat[:, :2 * S5_P]
    xf_ref[n_lat:, :] = r_ctx[:, :2 * S5_P]
    xb_ref[0:n_lat, :] = r_lat[:, 2 * S5_P:]
    xb_ref[n_lat:, :] = r_ctx[:, 2 * S5_P:]


def _s5_local(ug_lat, ug_ctx, wst):
    n_lat, n_ctx = ug_lat.shape[1], ug_ctx.shape[1]
    lh = S5_L * S5_H
    col = pl.BlockSpec((n_lat + n_ctx, 2 * S5_P), lambda g: (0, g))
    shape = jax.ShapeDtypeStruct((n_lat + n_ctx, S5_G * 2 * S5_P), F32)
    return pl.pallas_call(
        _s5_local_kernel,
        grid=(S5_G,),
        in_specs=[pl.BlockSpec((1, n_lat, lh), lambda g: (g, 0, 0)),
                  pl.BlockSpec((1, n_ctx, lh), lambda g: (g, 0, 0)),
                  pl.BlockSpec((1, lh, 4 * S5_P), lambda g: (g, 0, 0))],
        out_specs=[col, col],
        out_shape=[shape, shape],
        compiler_params=_params(("parallel",), 32),
        name="s5_local_state",
    )(ug_lat, ug_ctx, wst)


S5_SCAN_BLOCK = 8


def _s5_scan_kernel(xf_ref, xb_ref, a_ref, pf_ref, pb_ref, sf_ref, sb_ref):
    @pl.when(pl.program_id(0) == 0)
    def _():
        sf_ref[...] = jnp.zeros_like(sf_ref)
        sb_ref[...] = jnp.zeros_like(sb_ref)

    lanes = sf_ref.shape[1]
    in_re_half = (lax.broadcasted_iota(jnp.int32, (1, lanes), 1) & (2 * S5_P - 1)) < S5_P

    def step(s, x, aa, ab):
        partner = jnp.where(in_re_half, pltpu.roll(s, lanes - S5_P, axis=1), pltpu.roll(s, S5_P, axis=1))
        return aa * s + ab * partner + x

    s = sf_ref[...]
    for j in range(S5_SCAN_BLOCK):
        rows = slice(j * BATCH, (j + 1) * BATCH)
        pf_ref[rows, :] = s
        s = step(s, xf_ref[rows, :], a_ref[0], a_ref[1])
    sf_ref[...] = s
    s = sb_ref[...]
    for j in reversed(range(S5_SCAN_BLOCK)):
        rows = slice(j * BATCH, (j + 1) * BATCH)
        pb_ref[rows, :] = s
        s = step(s, xb_ref[rows, :], a_ref[2], a_ref[3])
    sb_ref[...] = s


def _s5_scan(xf, xb, a16):
    rows, lanes = xf.shape
    blk_rows = S5_SCAN_BLOCK * BATCH
    nblk = rows // blk_rows
    ctx_blk = (CTX_LEN // S5_L) // S5_SCAN_BLOCK
    lat_blk = nblk - ctx_blk
    fwd = lambda i: (jnp.where(i < ctx_blk, lat_blk + i, i - ctx_blk), 0)
    bwd = lambda i: (nblk - 1 - i, 0)
    blk = (blk_rows, lanes)
    shape = jax.ShapeDtypeStruct(xf.shape, F32)
    return pl.pallas_call(
        _s5_scan_kernel,
        grid=(nblk,),
        in_specs=[pl.BlockSpec(blk, fwd), pl.BlockSpec(blk, bwd),
                  pl.BlockSpec((4, 1, lanes), lambda i: (0, 0, 0))],
        out_specs=[pl.BlockSpec(blk, fwd), pl.BlockSpec(blk, bwd)],
        out_shape=[shape, shape],
        scratch_shapes=[pltpu.VMEM((BATCH, lanes), F32), pltpu.VMEM((BATCH, lanes), F32)],
        compiler_params=_params(("arbitrary",), 32),
        name="s5_chunk_scan",
    )(xf, xb, a16)


def _s5_out_kernel(pf_ref, pb_ref, u_ref, wc_ref, y_ref):
    dot = functools.partial(jnp.dot, precision=S5_PRECISION, preferred_element_type=F32)
    y = (dot(pf_ref[...], wc_ref[0, 0:2 * S5_P, :])
         + dot(pb_ref[...], wc_ref[0, 2 * S5_P:4 * S5_P, :])
         + dot(u_ref[0], wc_ref[0, 4 * S5_P:, :]))
    y_ref[0] = jax.nn.gelu(y)


def _s5_out(pf, pb, ug_lat, wc):
    rows = ug_lat.shape[1]
    lh = S5_L * S5_H
    col = pl.BlockSpec((rows, 2 * S5_P), lambda g: (0, g))
    return pl.pallas_call(
        _s5_out_kernel,
        grid=(S5_G,),
        in_specs=[col, col,
                  pl.BlockSpec((1, rows, lh), lambda g: (g, 0, 0)),
                  pl.BlockSpec((1, 4 * S5_P + lh, lh), lambda g: (g, 0, 0))],
        out_specs=pl.BlockSpec((1, rows, lh), lambda g: (g, 0, 0)),
        out_shape=jax.ShapeDtypeStruct((S5_G, rows, lh), F32),
        compiler_params=_params(("parallel",), 32),
        name="s5_output",
    )(pf, pb, ug_lat, wc)


def _s5_glu_kernel(y_ref, w_ref, o_ref):
    y = y_ref[...]
    gate = jnp.dot(y.astype(BF16), w_ref[...].astype(BF16), preferred_element_type=F32)
    o_ref[...] = (y * jax.nn.sigmoid(gate)).astype(o_ref.dtype)


def _s5_glu(y, w_glu, tm=1024):
    m = y.shape[0]
    return pl.pallas_call(
        _s5_glu_kernel,
        grid=(m // tm,),
        in_specs=[pl.BlockSpec((tm, D_A), lambda i: (i, 0)),
                  pl.BlockSpec((D_A, D_A), lambda i: (0, 0))],
        out_specs=pl.BlockSpec((tm, D_A), lambda i: (i, 0)),
        out_shape=jax.ShapeDtypeStruct((m, D_A), BF16),
        compiler_params=_params(("parallel",), 32),
        name="s5_glu",
    )(y, w_glu)


def _s5_branch(z, lam_re, lam_im, log_dt, b_re, b_im, c_re, c_im, d_skip, w_glu):
    wst, wc, a16 = _s5_prep(lam_re, lam_im, log_dt, b_re, b_im, c_re, c_im, d_skip)
    nc_ctx, nc_lat = CTX_LEN // S5_L, SEQ // S5_L
    n_chunks = nc_ctx + nc_lat
    u = z[:, :D_A]
    ug_ctx = u[N_LAT_ROWS:].reshape(BATCH, nc_ctx, S5_L, S5_G, S5_H).transpose(3, 1, 0, 2, 4)
    ug_ctx = ug_ctx.reshape(S5_G, nc_ctx * BATCH, S5_L * S5_H)
    ug_lat = u[:N_LAT_ROWS].reshape(BATCH, GRID_W // S5_L, S5_L, GRID_ROWS, S5_G, S5_H)
    ug_lat = ug_lat.transpose(4, 3, 1, 0, 2, 5).reshape(S5_G, nc_lat * BATCH, S5_L * S5_H)
    xf, xb = _s5_local(ug_lat, ug_ctx, wst)
    a16_lanes = jnp.swapaxes(a16, 0, 1).reshape(4, 1, S5_G * 2 * S5_P)
    pf, pb = _s5_scan(xf, xb, a16_lanes)
    yg = _s5_out(pf, pb, ug_lat, wc)
    y_lat = yg.reshape(S5_G, GRID_ROWS, GRID_W // S5_L, BATCH, S5_L, S5_H)
    y_lat = y_lat.transpose(3, 2, 4, 1, 0, 5).reshape(N_LAT_ROWS, D_A)
    return _s5_glu(y_lat, w_glu)


HG_PREP_ROWS = 256
HG_UNROLL = 12


def _chunk_cumsum(x, reverse):
    n = x.shape[0]
    pos = lax.broadcasted_iota(jnp.int32, (n, 1), 0) & (HG_CHUNK - 1)
    s = 1
    while s < HG_CHUNK:
        if reverse:
            x = x + jnp.where(pos < HG_CHUNK - s, pltpu.roll(x, n - s, axis=0), 0.0)
        else:
            x = x + jnp.where(pos >= s, pltpu.roll(x, s, axis=0), 0.0)
        s *= 2
    return x


def _hgrn_kernel(qc_ref, ql_ref, vc_ref, vl_ref, ffc_ref, ffl_ref, fbc_ref, fbl_ref, og_ref, lg_ref,
                 gn_ref, o_ref, qe_s, ke_s, qb_s, kd_s, v_s, dec_s, of_s, ob_s, *, layer):
    c = HG_CHUNK
    n_chunks = T_ALL // c
    n_ctx_chunks = CTX_LEN // c
    row = lax.broadcasted_iota(jnp.int32, (c, c), 0)
    col = lax.broadcasted_iota(jnp.int32, (c, c), 1)
    causal = row >= col
    anti = row <= col
    nt = (((1,), (1,)), ((), ()))
    tn = (((0,), (0,)), ((), ()))

    def lower_bound(d):
        logits = [lg_ref[d, j, 0] for j in range(lg_ref.shape[1])]
        top = functools.reduce(jnp.maximum, logits)
        e = [jnp.exp(l - top) for l in logits]
        return sum(e[:layer + 1]) / sum(e)

    lb = (lower_bound(0), lower_bound(1))

    def prepare(row0, chunk0, q, v, fpre):
        n = q.shape[0]
        rows = pl.ds(row0, n)
        as_chunks = lambda a: a.reshape(n // c, c, HG_DK)
        flat_bf16 = lambda a: a.reshape(n, HG_DK).astype(BF16)
        v_s[rows, :] = v.astype(BF16)
        q3 = as_chunks(q)
        for d in range(2):
            f = lb[d] + (1.0 - lb[d]) * jax.nn.sigmoid(fpre[d])
            k3 = as_chunks(1.0 - f)
            b3 = as_chunks(_chunk_cumsum(jnp.log(f), reverse=d == 1))
            mid = c // 2 - 1 if d == 0 else c // 2
            last = c - 1 if d == 0 else 0
            b_mid, b_last = b3[:, mid:mid + 1, :], b3[:, last:last + 1, :]
            qe_s[d, rows, :] = flat_bf16(q3 * jnp.exp(b3 - b_mid))
            ke_s[d, rows, :] = flat_bf16(k3 * jnp.exp(b_mid - b3))
            qb_s[d, rows, :] = flat_bf16(q3 * jnp.exp(b3))
            kd_s[d, rows, :] = flat_bf16(k3 * jnp.exp(b_last - b3))
            dec_s[d, pl.ds(chunk0, n // c)] = jnp.exp(b_last)

    prepare(0, 0, qc_ref[...], vc_ref[...], (ffc_ref[...], fbc_ref[...]))

    def prep_body(blk, carry):
        r0 = pl.multiple_of(blk * HG_PREP_ROWS, HG_PREP_ROWS)
        src = pl.ds(r0, HG_PREP_ROWS)
        prepare(pl.multiple_of(CTX_LEN + r0, HG_PREP_ROWS), n_ctx_chunks + blk * (HG_PREP_ROWS // c),
                ql_ref[src, :], vl_ref[src, :], (ffl_ref[src, :], fbl_ref[src, :]))
        return carry

    lax.fori_loop(0, SEQ // HG_PREP_ROWS, prep_body, 0)

    def chunk(cidx, d, state_t):
        rows = pl.ds(pl.multiple_of(cidx * c, c), c)
        v = v_s[rows, :]
        scores = lax.dot_general(qe_s[d, rows, :], ke_s[d, rows, :], nt, preferred_element_type=F32)
        scores = jnp.where(causal if d == 0 else anti, scores, 0.0)
        o = jnp.dot(scores.astype(BF16), v, preferred_element_type=F32)
        o = o + lax.dot_general(qb_s[d, rows, :], state_t.astype(BF16), nt, preferred_element_type=F32)
        kv_t = lax.dot_general(v, kd_s[d, rows, :], tn, preferred_element_type=F32)
        (of_s if d == 0 else ob_s)[rows, :] = o
        return state_t * dec_s[d, cidx] + kv_t

    def body(i, carry):
        sf, sb = carry
        sf = chunk(i, 0, sf)
        jb = jnp.where(i < n_ctx_chunks, n_ctx_chunks - 1 - i, n_chunks - 1 + n_ctx_chunks - i)
        sb = chunk(jb, 1, sb)
        return sf, sb

    zero = jnp.zeros((HG_DK, HG_DK), F32)
    lax.fori_loop(0, n_chunks, body, (zero, zero), unroll=HG_UNROLL)

    o = of_s[CTX_LEN:, :] + ob_s[CTX_LEN:, :]
    o = o * lax.rsqrt(jnp.mean(o * o, axis=-1, keepdims=True) + EPS)
    o_ref[...] = (o * gn_ref[0] * jax.nn.silu(og_ref[...])).astype(o_ref.dtype)


def _hgrn_branch(z, lb_logits, hg_norm, layer):
    n_slots = lb_logits.shape[1]
    ctx_blk0 = N_LAT_ROWS // CTX_LEN

    def windows(i):
        col_blk = (D_A + i * D_B) // HG_DK
        return [pl.BlockSpec((CTX_LEN, HG_DK), lambda b, h: (ctx_blk0 + b, col_blk + h)),
                pl.BlockSpec((SEQ, HG_DK), lambda b, h: (b, col_blk + h))]

    in_specs = [spec for i in range(4) for spec in windows(i)]
    in_specs += [windows(4)[1],
                 pl.BlockSpec((2, n_slots, 1, 1, HG_DK), lambda b, h: (0, 0, h, 0, 0)),
                 pl.BlockSpec((1, 1, HG_DK), lambda b, h: (h, 0, 0))]
    operand = lambda: pltpu.VMEM((2, T_ALL, HG_DK), BF16)
    lg = lb_logits.astype(F32).reshape(2, n_slots, HG_HEADS, 1, HG_DK)
    return pl.pallas_call(
        functools.partial(_hgrn_kernel, layer=layer),
        grid=(BATCH, HG_HEADS),
        in_specs=in_specs,
        out_specs=pl.BlockSpec((SEQ, HG_DK), lambda b, h: (b, h)),
        out_shape=jax.ShapeDtypeStruct((N_LAT_ROWS, D_B), BF16),
        scratch_shapes=[operand(), operand(), operand(), operand(),
                        pltpu.VMEM((T_ALL, HG_DK), BF16),
                        pltpu.VMEM((2, T_ALL // HG_CHUNK, 1, HG_DK), F32),
                        pltpu.VMEM((T_ALL, HG_DK), F32),
                        pltpu.VMEM((T_ALL, HG_DK), F32)],
        compiler_params=_params(("parallel", "parallel"), 40),
        name="hgrn2",
    )(*([z] * 9), lg, hg_norm.reshape(HG_HEADS, 1, HG_DK))


FF_DOWN_TILE = 512


def kernel(x, c, ctx, c_ctx, w_ada, b_ada, norm_ffn1, w1_ffn1, w3_ffn1, w2_ffn1, norm_mix, w_in,
           s5_lam_re, s5_lam_im, s5_log_dt, s5_b_re, s5_b_im, s5_c_re, s5_c_im, s5_d, s5_w_glu,
           hg_lb_logits, hg_norm, w_proj_a, w_proj_b, w_out, norm_ffn2, w1_ffn2, w3_ffn2, w2_ffn2,
           norm_final):
    layer = 0
    c8 = jnp.concatenate([c, c_ctx[None], jnp.zeros((8 - BATCH - 1, D_MODEL), F32)], axis=0)
    mod = _ada(c8, w_ada[layer], b_ada[layer]).reshape(8, N_MOD, D_MODEL)
    mod = jnp.concatenate([mod[BATCH:BATCH + 1], mod[:BATCH]], axis=0)
    m = [mod[:, i].reshape(BATCH + 1, 1, D_MODEL) for i in range(N_MOD)]

    x_scan = x.reshape(BATCH, GRID_ROWS, GRID_W, D_MODEL).swapaxes(1, 2).reshape(N_LAT_ROWS, D_MODEL)
    h0 = jnp.concatenate([x_scan, ctx.reshape(N_CTX_ROWS, D_MODEL)], axis=0)
    tile = 1024

    z1 = _norm_mod(h0, norm_ffn1[layer], m[0], m[1], _group_full(512))
    g1, w2_bf16 = _swiglu_up(z1, w1_ffn1[layer], w3_ffn1[layer], w2_ffn1[layer])
    h1 = _mm_res(g1, w2_bf16, h0, 0, m[2], 0.5, _group_full(FF_DOWN_TILE),
                 FF_DOWN_TILE, FF_DOWN_TILE, D_FF, 56)

    z2 = _norm_mod(h1, norm_mix[layer], m[3], m[4], _group_full(512))
    z = _mm(z2, w_in[layer], F32, tile, 512, 52, "mixer_in_proj")

    y_a = _s5_branch(z, s5_lam_re[layer], s5_lam_im[layer], s5_log_dt[layer], s5_b_re[layer],
                     s5_b_im[layer], s5_c_re[layer], s5_c_im[layer], s5_d[layer], s5_w_glu[layer])
    y_b = _hgrn_branch(z, hg_lb_logits, hg_norm[layer], layer)

    ga_tile = (D_A + 5 * D_B) // 512
    merged = _merge(y_a, w_proj_a[layer], y_b, w_proj_b[layer], z,
                    ga_tile, ga_tile + D_MODEL // 512, 0)
    h2 = _mm_res(merged, w_out[layer], h1, 0, m[5], 1.0, _group_lat(tile),
                 tile, 512, D_MODEL, 52)

    z3 = _norm_mod(h2, norm_ffn2[layer], m[6], m[7], _group_lat(512))
    g2, w2_bf16 = _swiglu_up(z3, w1_ffn2[layer], w3_ffn2[layer], w2_ffn2[layer])
    h3 = _mm_res(g2, w2_bf16, h2, 0, m[8], 0.5, _group_lat(FF_DOWN_TILE),
                 FF_DOWN_TILE, FF_DOWN_TILE, D_FF, 56)
    out = _final_norm(h3, norm_final).reshape(BATCH, GRID_W, GRID_ROWS, D_MODEL)
    return out.swapaxes(1, 2).reshape(BATCH, SEQ, D_MODEL)
```

```python
import functools
import math

import jax
import jax.numpy as jnp
from jax import lax
from jax.experimental import pallas as pl
from jax.experimental.pallas import tpu as pltpu

F32 = jnp.float32
BF16 = jnp.bfloat16

D_MODEL = 4096
BATCH = 4
SEQ = 2048
GRID_W = 64
GRID_ROWS = SEQ // GRID_W
CTX_LEN = 256
T_ALL = CTX_LEN + SEQ
D_FF = 11008
D_A = 1024
S5_H = 16
S5_G = D_A // S5_H
S5_P = 64
S5_L = 16
D_B = 2048
HG_DK = 128
HG_HEADS = D_B // HG_DK
HG_CHUNK = 64
N_MOD = 9
N_IN = D_A + 5 * D_B + 2 * D_MODEL
EPS = 1e-6

N_CTX_ROWS = BATCH * CTX_LEN
N_LAT_ROWS = BATCH * SEQ
N_ROWS = N_CTX_ROWS + N_LAT_ROWS
MIB = 1024 * 1024


def _params(semantics, vmem_mib):
    return pltpu.CompilerParams(dimension_semantics=semantics, vmem_limit_bytes=vmem_mib * MIB)


def _group_full(tm):
    n_lat_tiles, per_batch = N_LAT_ROWS // tm, SEQ // tm
    return lambda i: jnp.where(i < n_lat_tiles, 1 + i // per_batch, 0)


def _group_lat(tm):
    per_batch = SEQ // tm
    return lambda i: 1 + i // per_batch


def _ada_kernel(c_ref, w_ref, b_ref, o_ref):
    sc = jax.nn.silu(c_ref[...]).astype(BF16)
    o_ref[...] = jnp.dot(sc, w_ref[...].astype(BF16), preferred_element_type=F32) + b_ref[...]


def _ada(c8, w_ada, b_ada):
    n = w_ada.shape[1]
    tn = 1024
    return pl.pallas_call(
        _ada_kernel,
        grid=(n // tn,),
        in_specs=[pl.BlockSpec((8, D_MODEL), lambda j: (0, 0)),
                  pl.BlockSpec((D_MODEL, tn), lambda j: (0, j)),
                  pl.BlockSpec((1, tn), lambda j: (0, j))],
        out_specs=pl.BlockSpec((8, tn), lambda j: (0, j)),
        out_shape=jax.ShapeDtypeStruct((8, n), F32),
        compiler_params=_params(("parallel",), 48),
        name="ada_table",
    )(c8, w_ada, b_ada.reshape(1, n))


def _norm_mod_kernel(x_ref, g_ref, sh_ref, sc_ref, o_ref):
    x = x_ref[...]
    var = jnp.mean(x * x, axis=-1, keepdims=True)
    y = x * lax.rsqrt(var + EPS) * g_ref[...]
    o_ref[...] = (y * (1.0 + sc_ref[0]) + sh_ref[0]).astype(o_ref.dtype)


def _norm_mod(x, gain, shift, scale, group_fn, tm=512):
    rows = x.shape[0]
    vec = pl.BlockSpec((1, 1, D_MODEL), lambda i: (group_fn(i), 0, 0))
    return pl.pallas_call(
        _norm_mod_kernel,
        grid=(rows // tm,),
        in_specs=[pl.BlockSpec((tm, D_MODEL), lambda i: (i, 0)),
                  pl.BlockSpec((1, D_MODEL), lambda i: (0, 0)),
                  vec, vec],
        out_specs=pl.BlockSpec((tm, D_MODEL), lambda i: (i, 0)),
        out_shape=jax.ShapeDtypeStruct((rows, D_MODEL), BF16),
        compiler_params=_params(("parallel",), 40),
        name="norm_modulate",
    )(x, gain.reshape(1, D_MODEL), shift, scale)


ENTRY_COLS = 8


def _entry_kernel(x_ref, c_ref, g_ref, sh_ref, sc_ref, z_ref, h_ref, *, n_lat_tiles):
    def emit(rows, x):
        var = jnp.mean(x * x, axis=-1, keepdims=True)
        y = x * lax.rsqrt(var + EPS) * g_ref[...]
        z_ref[rows, :] = (y * (1.0 + sc_ref[0]) + sh_ref[0]).astype(z_ref.dtype)
        h_ref[rows, :] = x

    @pl.when(pl.program_id(0) < n_lat_tiles)
    def _():
        for col in range(ENTRY_COLS):
            emit(slice(col * GRID_ROWS, (col + 1) * GRID_ROWS), x_ref[0, :, col, :])

    @pl.when(pl.program_id(0) >= n_lat_tiles)
    def _():
        emit(slice(None), c_ref[...])


def _entry(x, ctx, gain, shift, scale):
    tm = ENTRY_COLS * GRID_ROWS
    tiles_per_batch = GRID_W // ENTRY_COLS
    n_lat_tiles = BATCH * tiles_per_batch
    lat = lambda i: jnp.minimum(i, n_lat_tiles - 1)
    group = lambda i: jnp.where(i < n_lat_tiles, 1 + i // tiles_per_batch, 0)
    vec = pl.BlockSpec((1, 1, D_MODEL), lambda i: (group(i), 0, 0))
    rows = pl.BlockSpec((tm, D_MODEL), lambda i: (i, 0))
    return pl.pallas_call(
        functools.partial(_entry_kernel, n_lat_tiles=n_lat_tiles),
        grid=(N_ROWS // tm,),
        in_specs=[pl.BlockSpec((1, GRID_ROWS, ENTRY_COLS, D_MODEL),
                               lambda i: (lat(i) // tiles_per_batch, 0, lat(i) % tiles_per_batch, 0)),
                  pl.BlockSpec((tm, D_MODEL), lambda i: (jnp.maximum(i - n_lat_tiles, 0), 0)),
                  pl.BlockSpec((1, D_MODEL), lambda i: (0, 0)),
                  vec, vec],
        out_specs=[rows, rows],
        out_shape=[jax.ShapeDtypeStruct((N_ROWS, D_MODEL), BF16),
                   jax.ShapeDtypeStruct((N_ROWS, D_MODEL), F32)],
        compiler_params=_params(("parallel",), 40),
        name="entry_norm_modulate",
    )(x.reshape(BATCH, GRID_ROWS, GRID_W, D_MODEL), ctx.reshape(N_CTX_ROWS, D_MODEL),
      gain.reshape(1, D_MODEL), shift, scale)


def _final_norm_kernel(x_ref, g_ref, o_ref):
    x = x_ref[...]
    var = jnp.mean(x * x, axis=-1, keepdims=True)
    o_ref[...] = x * lax.rsqrt(var + EPS) * g_ref[...]


def _final_norm(x, gain, tm=512):
    rows = x.shape[0]
    return pl.pallas_call(
        _final_norm_kernel,
        grid=(rows // tm,),
        in_specs=[pl.BlockSpec((tm, D_MODEL), lambda i: (i, 0)),
                  pl.BlockSpec((1, D_MODEL), lambda i: (0, 0))],
        out_specs=pl.BlockSpec((tm, D_MODEL), lambda i: (i, 0)),
        out_shape=jax.ShapeDtypeStruct((rows, D_MODEL), F32),
        compiler_params=_params(("parallel",), 40),
        name="final_norm",
    )(x, gain.reshape(1, D_MODEL))


W2_CAST_ROWS = 64


def _swiglu_up_kernel(z_ref, w1_ref, w3_ref, w2_ref, o_ref, w2_bf16_ref):
    z = z_ref[...]
    h1 = jnp.dot(z, w1_ref[...].astype(BF16), preferred_element_type=F32)
    h3 = jnp.dot(z, w3_ref[...].astype(BF16), preferred_element_type=F32)
    o_ref[...] = (jax.nn.silu(h1) * h3).astype(o_ref.dtype)
    w2_bf16_ref[...] = w2_ref[...].astype(BF16)


def _swiglu_up(z, w1, w3, w2, tm=1024, tn=512):
    m, k = z.shape
    n = w1.shape[1]
    nj = pl.cdiv(n, tn)
    n_w2_blocks = w2.shape[0] // W2_CAST_ROWS
    assert (m // tm) * nj >= n_w2_blocks and w2.shape[0] % W2_CAST_ROWS == 0
    w2_spec = pl.BlockSpec((W2_CAST_ROWS, w2.shape[1]),
                           lambda i, j: (jnp.minimum(i * nj + j, n_w2_blocks - 1), 0))
    return pl.pallas_call(
        _swiglu_up_kernel,
        grid=(m // tm, nj),
        in_specs=[pl.BlockSpec((tm, k), lambda i, j: (i, 0), pipeline_mode=pl.Buffered(1)),
                  pl.BlockSpec((k, tn), lambda i, j: (0, j)),
                  pl.BlockSpec((k, tn), lambda i, j: (0, j)),
                  w2_spec],
        out_specs=[pl.BlockSpec((tm, tn), lambda i, j: (i, j)), w2_spec],
        out_shape=[jax.ShapeDtypeStruct((m, n), BF16), jax.ShapeDtypeStruct(w2.shape, BF16)],
        compiler_params=_params(("arbitrary", "arbitrary"), 56),
        name="swiglu_up",
    )(z, w1, w3, w2)


def _mm_res_kernel(a_ref, w_ref, res_ref, gate_ref, o_ref, *scratch, scale, nk):
    if nk == 1:
        prod = jnp.dot(a_ref[...], w_ref[...].astype(BF16), preferred_element_type=F32)
        o_ref[...] = res_ref[...] + (scale * gate_ref[0]) * prod
        return
    acc_ref, = scratch
    k = pl.program_id(2)

    @pl.when(k == 0)
    def _():
        acc_ref[...] = jnp.zeros_like(acc_ref)

    acc_ref[...] += jnp.dot(a_ref[...], w_ref[...].astype(BF16), preferred_element_type=F32)

    @pl.when(k == nk - 1)
    def _():
        o_ref[...] = res_ref[...] + (scale * gate_ref[0]) * acc_ref[...]


def _mm_res(a, w, res, res_tile_off, gate, scale, group_fn, tm, tn, tk, vmem_mib):
    m, k = a.shape
    n = w.shape[1]
    nk = k // tk
    return pl.pallas_call(
        functools.partial(_mm_res_kernel, scale=scale, nk=nk),
        grid=(m // tm, n // tn, nk),
        in_specs=[pl.BlockSpec((tm, tk), lambda i, j, kk: (i, kk)),
                  pl.BlockSpec((tk, tn), lambda i, j, kk: (kk, j)),
                  pl.BlockSpec((tm, tn), lambda i, j, kk: (i + res_tile_off, j)),
                  pl.BlockSpec((1, 1, tn), lambda i, j, kk: (group_fn(i), 0, j))],
        out_specs=pl.BlockSpec((tm, tn), lambda i, j, kk: (i, j)),
        out_shape=jax.ShapeDtypeStruct((m, n), F32),
        scratch_shapes=[] if nk == 1 else [pltpu.VMEM((tm, tn), F32)],
        compiler_params=_params(("parallel", "parallel", "arbitrary"), vmem_mib),
        name="matmul_gated_residual",
    )(a, w, res, gate)


def _mm_kernel(a_ref, w_ref, o_ref):
    o_ref[...] = jnp.dot(a_ref[...], w_ref[...].astype(BF16), preferred_element_type=F32).astype(o_ref.dtype)


def _mm(a, w, out_dtype, tm, tn, vmem_mib, name):
    m, k = a.shape
    n = w.shape[1]
    return pl.pallas_call(
        _mm_kernel,
        grid=(m // tm, n // tn),
        in_specs=[pl.BlockSpec((tm, k), lambda i, j: (i, 0)),
                  pl.BlockSpec((k, tn), lambda i, j: (0, j))],
        out_specs=pl.BlockSpec((tm, tn), lambda i, j: (i, j)),
        out_shape=jax.ShapeDtypeStruct((m, n), out_dtype),
        compiler_params=_params(("parallel", "arbitrary"), vmem_mib),
        name=name,
    )(a, w)


def _merge_kernel(ya_ref, wa_ref, yb_ref, wb_ref, ga_ref, gb_ref, o_ref):
    pa = jnp.dot(ya_ref[...], wa_ref[...].astype(BF16), preferred_element_type=F32)
    pb = jnp.dot(yb_ref[...], wb_ref[...].astype(BF16), preferred_element_type=F32)
    o_ref[...] = (jax.nn.sigmoid(ga_ref[...]) * pa + jax.nn.sigmoid(gb_ref[...]) * pb).astype(o_ref.dtype)


def _merge(ya, wa, yb, wb, z, ga_col_tile, gb_col_tile, z_row_tile_off, tm=1024, tn=512):
    m = ya.shape[0]
    n = wa.shape[1]
    return pl.pallas_call(
        _merge_kernel,
        grid=(m // tm, n // tn),
        in_specs=[pl.BlockSpec((tm, D_A), lambda i, j: (i, 0)),
                  pl.BlockSpec((D_A, tn), lambda i, j: (0, j)),
                  pl.BlockSpec((tm, D_B), lambda i, j: (i, 0)),
                  pl.BlockSpec((D_B, tn), lambda i, j: (0, j)),
                  pl.BlockSpec((tm, tn), lambda i, j: (i + z_row_tile_off, ga_col_tile + j)),
                  pl.BlockSpec((tm, tn), lambda i, j: (i + z_row_tile_off, gb_col_tile + j))],
        out_specs=pl.BlockSpec((tm, tn), lambda i, j: (i, j)),
        out_shape=jax.ShapeDtypeStruct((m, n), BF16),
        compiler_params=_params(("parallel", "arbitrary"), 48),
        name="gated_merge",
    )(ya, wa, yb, wb, z, z)


S5_PRECISION = lax.Precision.HIGHEST


def _cmul(a, b):
    return a[0] * b[0] - a[1] * b[1], a[0] * b[1] + a[1] * b[0]


def _s5_prep_kernel(ldt_ref, lam_row_ref, lam_col_ref, b1_ref, b2_ref, c1_ref, c2_ref, dt_ref,
                    wst_ref, wc_ref, a16_ref):
    g = pl.program_id(0)
    lh = S5_L * S5_H
    lane_sign = jnp.where(lax.broadcasted_iota(jnp.int32, (1, 2 * S5_P), 1) < S5_P, -1.0, 1.0)
    row_sign = jnp.where(lax.broadcasted_iota(jnp.int32, (2 * S5_P, 1), 0) < S5_P, 1.0, -1.0)
    log2_h = S5_H.bit_length() - 1
    log2_l = S5_L.bit_length() - 1
    t_idx = lax.shift_right_logical(lax.broadcasted_iota(jnp.int32, (1, lh), 1), log2_h)
    b1 = b1_ref[0]
    b2 = b2_ref[0]
    c1 = c1_ref[0]
    c2 = c2_ref[0]
    toep = None
    for d in range(2):
        dt = jnp.exp(jnp.full((1, 1), ldt_ref[d, g], F32))
        lr = lam_row_ref[d, 0]
        li = lam_row_ref[2 + d, 0]
        mag = jnp.exp(lr * dt)
        ab_re, ab_im = mag * jnp.cos(li * dt), mag * jnp.sin(li * dt)
        den = lr * lr + li * li
        nr = ab_re - 1.0
        kr = (nr * lr + ab_im * li) / den
        ki = (ab_im * lr - nr * li) / den
        bk = kr * b1 + lane_sign * ki * b2
        bk_sw = lane_sign * (kr * b2 - lane_sign * ki * b1)
        pw = [(jnp.ones_like(ab_re), jnp.zeros_like(ab_re))]
        for _ in range(S5_L):
            pw.append(_cmul(pw[-1], (ab_re, ab_im)))
        taus = range(S5_L - 1, -1, -1) if d == 0 else range(S5_L)
        rows_of = lambda part: jnp.concatenate(
            [jnp.broadcast_to(pw[tau][part], (S5_H, 2 * S5_P)) for tau in taus], axis=0)
        wst_ref[0, :, d * 2 * S5_P:(d + 1) * 2 * S5_P] = rows_of(0) * bk + rows_of(1) * bk_sw
        a16_ref[0, 2 * d:2 * d + 1, :] = pw[S5_L][0]
        a16_ref[0, 2 * d + 1:2 * d + 2, :] = lane_sign * pw[S5_L][1]
        lrc = lam_col_ref[d, 0]
        lic = lam_col_ref[2 + d, 0]
        magc = jnp.exp(lrc * dt)
        squares = [(magc * jnp.cos(lic * dt), magc * jnp.sin(lic * dt))]
        for _ in range(log2_l - 1):
            squares.append(_cmul(squares[-1], squares[-1]))
        tau_col = t_idx if d == 0 else (S5_L - 1) - t_idx
        q0 = None
        for j, sq in enumerate(squares):
            bit = (lax.shift_right_logical(tau_col, j) & 1) == 1
            factor = (jnp.where(bit, sq[0], 1.0), jnp.where(bit, sq[1], 0.0))
            q0 = factor if q0 is None else _cmul(q0, factor)
        q1 = _cmul(q0, squares[0])

        def c_times(q):
            return row_sign * c1 * q[0] - c2 * q[1]

        wc_ref[0, d * 2 * S5_P:(d + 1) * 2 * S5_P, :] = c_times(q1)
        m_all = jnp.dot(bk[0:S5_H], c_times(q0), precision=S5_PRECISION,
                        preferred_element_type=F32)
        lane = lax.broadcasted_iota(jnp.int32, (S5_H, lh), 1)
        blocks = []
        for s in range(S5_L):
            if d == 0:
                shift, keep = S5_H * s, lane >= S5_H * s
            else:
                shift, keep = (S5_H * (s + 1)) % lh, lane < S5_H * (s + 1)
            rolled = m_all if shift == 0 else pltpu.roll(m_all, shift, axis=1)
            blocks.append(jnp.where(keep, rolled, 0.0))
        td = jnp.concatenate(blocks, axis=0)
        toep = td if toep is None else toep + td
    diag = lax.broadcasted_iota(jnp.int32, (lh, lh), 0) == lax.broadcasted_iota(jnp.int32, (lh, lh), 1)
    wc_ref[0, 4 * S5_P:, :] = toep + jnp.where(diag, dt_ref[0], 0.0)


def _s5_prep(lam_re, lam_im, log_dt, b_re, b_im, c_re, c_im, d_skip):
    lh = S5_L * S5_H
    dup = lambda z: jnp.concatenate([z, z], axis=-1)
    lam = jnp.concatenate([lam_re, lam_im], axis=0)
    lam_row = dup(lam)[:, :, None, :]
    lam_col = dup(lam)[:, :, :, None]
    bt_re, bt_im = jnp.swapaxes(b_re, 1, 2), jnp.swapaxes(b_im, 1, 2)
    b1 = jnp.tile(jnp.concatenate([bt_re, bt_im], axis=-1), (1, S5_L, 1))
    b2 = jnp.tile(jnp.concatenate([bt_im, bt_re], axis=-1), (1, S5_L, 1))
    ct_re, ct_im = jnp.swapaxes(c_re, 1, 2), jnp.swapaxes(c_im, 1, 2)
    c1 = jnp.tile(jnp.concatenate([ct_re, ct_im], axis=1), (1, 1, S5_L))
    c2 = jnp.tile(jnp.concatenate([ct_im, ct_re], axis=1), (1, 1, S5_L))
    d_tiled = jnp.tile(d_skip.reshape(S5_G, 1, S5_H), (1, 1, S5_L))
    grp = lambda shape: pl.BlockSpec((1,) + shape, lambda g: (g, 0, 0))
    return pl.pallas_call(
        _s5_prep_kernel,
        grid=(S5_G,),
        in_specs=[pl.BlockSpec(memory_space=pltpu.SMEM),
                  pl.BlockSpec((4, 1, 1, 2 * S5_P), lambda g: (0, g, 0, 0)),
                  pl.BlockSpec((4, 1, 2 * S5_P, 1), lambda g: (0, g, 0, 0)),
                  grp((lh, 2 * S5_P)), grp((lh, 2 * S5_P)),
                  grp((2 * S5_P, lh)), grp((2 * S5_P, lh)),
                  grp((1, lh))],
        out_specs=[grp((lh, 4 * S5_P)), grp((4 * S5_P + lh, lh)), grp((4, 2 * S5_P))],
        out_shape=[jax.ShapeDtypeStruct((S5_G, lh, 4 * S5_P), F32),
                   jax.ShapeDtypeStruct((S5_G, 4 * S5_P + lh, lh), F32),
                   jax.ShapeDtypeStruct((S5_G, 4, 2 * S5_P), F32)],
        compiler_params=_params(("parallel",), 32),
        name="s5_prep",
    )(log_dt, lam_row, lam_col, b1, b2, c1, c2, d_tiled)


def _s5_local_kernel(ul_ref, uc_ref, wst_ref, xf_ref, xb_ref):
    n_lat = ul_ref.shape[1]
    dot = functools.partial(jnp.dot, precision=S5_PRECISION, preferred_element_type=F32)
    r_lat = dot(ul_ref[0], wst_ref[0])
    r_ctx = dot(uc_ref[0], wst_ref[0])
    xf_ref[0:n_lat, :] = r_lat[:, :2 * S5_P]
    xf_ref[n_lat:, :] = r_ctx[:, :2 * S5_P]
    xb_ref[0:n_lat, :] = r_lat[:, 2 * S5_P:]
    xb_ref[n_lat:, :] = r_ctx[:, 2 * S5_P:]


def _s5_local(ug_lat, ug_ctx, wst):
    n_lat, n_ctx = ug_lat.shape[1], ug_ctx.shape[1]
    lh = S5_L * S5_H
    col = pl.BlockSpec((n_lat + n_ctx, 2 * S5_P), lambda g: (0, g))
    shape = jax.ShapeDtypeStruct((n_lat + n_ctx, S5_G * 2 * S5_P), F32)
    return pl.pallas_call(
        _s5_local_kernel,
        grid=(S5_G,),
        in_specs=[pl.BlockSpec((1, n_lat, lh), lambda g: (g, 0, 0)),
                  pl.BlockSpec((1, n_ctx, lh), lambda g: (g, 0, 0)),
                  pl.BlockSpec((1, lh, 4 * S5_P), lambda g: (g, 0, 0))],
        out_specs=[col, col],
        out_shape=[shape, shape],
        compiler_params=_params(("parallel",), 32),
        name="s5_local_state",
    )(ug_lat, ug_ctx, wst)


S5_SCAN_BLOCK = 8


def _s5_scan_kernel(xf_ref, xb_ref, a_ref, pf_ref, pb_ref, sf_ref, sb_ref):
    @pl.when(pl.program_id(0) == 0)
    def _():
        sf_ref[...] = jnp.zeros_like(sf_ref)
        sb_ref[...] = jnp.zeros_like(sb_ref)

    lanes = sf_ref.shape[1]
    in_re_half = (lax.broadcasted_iota(jnp.int32, (1, lanes), 1) & (2 * S5_P - 1)) < S5_P

    def step(s, x, aa, ab):
        partner = jnp.where(in_re_half, pltpu.roll(s, lanes - S5_P, axis=1), pltpu.roll(s, S5_P, axis=1))
        return aa * s + ab * partner + x

    s = sf_ref[...]
    for j in range(S5_SCAN_BLOCK):
        rows = slice(j * BATCH, (j + 1) * BATCH)
        pf_ref[rows, :] = s
        s = step(s, xf_ref[rows, :], a_ref[0], a_ref[1])
    sf_ref[...] = s
    s = sb_ref[...]
    for j in reversed(range(S5_SCAN_BLOCK)):
        rows = slice(j * BATCH, (j + 1) * BATCH)
        pb_ref[rows, :] = s
        s = step(s, xb_ref[rows, :], a_ref[2], a_ref[3])
    sb_ref[...] = s


def _s5_scan(xf, xb, a16):
    rows, lanes = xf.shape
    blk_rows = S5_SCAN_BLOCK * BATCH
    nblk = rows // blk_rows
    ctx_blk = (CTX_LEN // S5_L) // S5_SCAN_BLOCK
    lat_blk = nblk - ctx_blk
    fwd = lambda i: (jnp.where(i < ctx_blk, lat_blk + i, i - ctx_blk), 0)
    bwd = lambda i: (nblk - 1 - i, 0)
    blk = (blk_rows, lanes)
    shape = jax.ShapeDtypeStruct(xf.shape, F32)
    return pl.pallas_call(
        _s5_scan_kernel,
        grid=(nblk,),
        in_specs=[pl.BlockSpec(blk, fwd), pl.BlockSpec(blk, bwd),
                  pl.BlockSpec((4, 1, lanes), lambda i: (0, 0, 0))],
        out_specs=[pl.BlockSpec(blk, fwd), pl.BlockSpec(blk, bwd)],
        out_shape=[shape, shape],
        scratch_shapes=[pltpu.VMEM((BATCH, lanes), F32), pltpu.VMEM((BATCH, lanes), F32)],
        compiler_params=_params(("arbitrary",), 32),
        name="s5_chunk_scan",
    )(xf, xb, a16)


def _s5_out_kernel(pf_ref, pb_ref, u_ref, wc_ref, y_ref):
    dot = functools.partial(jnp.dot, precision=S5_PRECISION, preferred_element_type=F32)
    y = (dot(pf_ref[...], wc_ref[0, 0:2 * S5_P, :])
         + dot(pb_ref[...], wc_ref[0, 2 * S5_P:4 * S5_P, :])
         + dot(u_ref[0], wc_ref[0, 4 * S5_P:, :]))
    y_ref[0] = jax.nn.gelu(y)


def _s5_out(pf, pb, ug_lat, wc):
    rows = ug_lat.shape[1]
    lh = S5_L * S5_H
    col = pl.BlockSpec((rows, 2 * S5_P), lambda g: (0, g))
    return pl.pallas_call(
        _s5_out_kernel,
        grid=(S5_G,),
        in_specs=[col, col,
                  pl.BlockSpec((1, rows, lh), lambda g: (g, 0, 0)),
                  pl.BlockSpec((1, 4 * S5_P + lh, lh), lambda g: (g, 0, 0))],
        out_specs=pl.BlockSpec((1, rows, lh), lambda g: (g, 0, 0)),
        out_shape=jax.ShapeDtypeStruct((S5_G, rows, lh), F32),
        compiler_params=_params(("parallel",), 32),
        name="s5_output",
    )(pf, pb, ug_lat, wc)


def _s5_glu_kernel(y_ref, w_ref, o_ref):
    y = y_ref[...]
    gate = jnp.dot(y.astype(BF16), w_ref[...].astype(BF16), preferred_element_type=F32)
    o_ref[...] = (y * jax.nn.sigmoid(gate)).astype(o_ref.dtype)


def _s5_glu(y, w_glu, tm=1024):
    m = y.shape[0]
    return pl.pallas_call(
        _s5_glu_kernel,
        grid=(m // tm,),
        in_specs=[pl.BlockSpec((tm, D_A), lambda i: (i, 0)),
                  pl.BlockSpec((D_A, D_A), lambda i: (0, 0))],
        out_specs=pl.BlockSpec((tm, D_A), lambda i: (i, 0)),
        out_shape=jax.ShapeDtypeStruct((m, D_A), BF16),
        compiler_params=_params(("parallel",), 32),
        name="s5_glu",
    )(y, w_glu)


def _s5_branch(z, lam_re, lam_im, log_dt, b_re, b_im, c_re, c_im, d_skip, w_glu):
    wst, wc, a16 = _s5_prep(lam_re, lam_im, log_dt, b_re, b_im, c_re, c_im, d_skip)
    nc_ctx, nc_lat = CTX_LEN // S5_L, SEQ // S5_L
    n_chunks = nc_ctx + nc_lat
    u = z[:, :D_A]
    ug_ctx = u[N_LAT_ROWS:].reshape(BATCH, nc_ctx, S5_L, S5_G, S5_H).transpose(3, 1, 0, 2, 4)
    ug_ctx = ug_ctx.reshape(S5_G, nc_ctx * BATCH, S5_L * S5_H)
    ug_lat = u[:N_LAT_ROWS].reshape(BATCH, GRID_W // S5_L, S5_L, GRID_ROWS, S5_G, S5_H)
    ug_lat = ug_lat.transpose(4, 3, 1, 0, 2, 5).reshape(S5_G, nc_lat * BATCH, S5_L * S5_H)
    xf, xb = _s5_local(ug_lat, ug_ctx, wst)
    a16_lanes = jnp.swapaxes(a16, 0, 1).reshape(4, 1, S5_G * 2 * S5_P)
    pf, pb = _s5_scan(xf, xb, a16_lanes)
    yg = _s5_out(pf, pb, ug_lat, wc)
    y_lat = yg.reshape(S5_G, GRID_ROWS, GRID_W // S5_L, BATCH, S5_L, S5_H)
    y_lat = y_lat.transpose(3, 2, 4, 1, 0, 5).reshape(N_LAT_ROWS, D_A)
    return _s5_glu(y_lat, w_glu)


HG_PREP_ROWS = 256
HG_UNROLL = 12


def _chunk_cumsum(x, reverse):
    n = x.shape[0]
    pos = lax.broadcasted_iota(jnp.int32, (n, 1), 0) & (HG_CHUNK - 1)
    s = 1
    while s < HG_CHUNK:
        if reverse:
            x = x + jnp.where(pos < HG_CHUNK - s, pltpu.roll(x, n - s, axis=0), 0.0)
        else:
            x = x + jnp.where(pos >= s, pltpu.roll(x, s, axis=0), 0.0)
        s *= 2
    return x


def _hgrn_kernel(qc_ref, ql_ref, vc_ref, vl_ref, ffc_ref, ffl_ref, fbc_ref, fbl_ref, og_ref, lg_ref,
                 gn_ref, o_ref, qe_s, ke_s, qb_s, kd_s, v_s, dec_s, of_s, ob_s, *, layer):
    c = HG_CHUNK
    n_chunks = T_ALL // c
    n_ctx_chunks = CTX_LEN // c
    row = lax.broadcasted_iota(jnp.int32, (c, c), 0)
    col = lax.broadcasted_iota(jnp.int32, (c, c), 1)
    causal = row >= col
    anti = row <= col
    nt = (((1,), (1,)), ((), ()))
    tn = (((0,), (0,)), ((), ()))

    def lower_bound(d):
        logits = [lg_ref[d, j, 0] for j in range(lg_ref.shape[1])]
        top = functools.reduce(jnp.maximum, logits)
        e = [jnp.exp(l - top) for l in logits]
        return sum(e[:layer + 1]) / sum(e)

    lb = (lower_bound(0), lower_bound(1))

    def prepare(row0, chunk0, q, v, fpre):
        n = q.shape[0]
        rows = pl.ds(row0, n)
        as_chunks = lambda a: a.reshape(n // c, c, HG_DK)
        flat_bf16 = lambda a: a.reshape(n, HG_DK).astype(BF16)
        v_s[rows, :] = v.astype(BF16)
        q3 = as_chunks(q)
        for d in range(2):
            f = lb[d] + (1.0 - lb[d]) * jax.nn.sigmoid(fpre[d])
            k3 = as_chunks(1.0 - f)
            b3 = as_chunks(_chunk_cumsum(jnp.log(f), reverse=d == 1))
            mid = c // 2 - 1 if d == 0 else c // 2
            last = c - 1 if d == 0 else 0
            b_mid, b_last = b3[:, mid:mid + 1, :], b3[:, last:last + 1, :]
            qe_s[d, rows, :] = flat_bf16(q3 * jnp.exp(b3 - b_mid))
            ke_s[d, rows, :] = flat_bf16(k3 * jnp.exp(b_mid - b3))
            qb_s[d, rows, :] = flat_bf16(q3 * jnp.exp(b3))
            kd_s[d, rows, :] = flat_bf16(k3 * jnp.exp(b_last - b3))
            dec_s[d, pl.ds(chunk0, n // c)] = jnp.exp(b_last)

    prepare(0, 0, qc_ref[...], vc_ref[...], (ffc_ref[...], fbc_ref[...]))

    def prep_body(blk, carry):
        r0 = pl.multiple_of(blk * HG_PREP_ROWS, HG_PREP_ROWS)
        src = pl.ds(r0, HG_PREP_ROWS)
        prepare(pl.multiple_of(CTX_LEN + r0, HG_PREP_ROWS), n_ctx_chunks + blk * (HG_PREP_ROWS // c),
                ql_ref[src, :], vl_ref[src, :], (ffl_ref[src, :], fbl_ref[src, :]))
        return carry

    lax.fori_loop(0, SEQ // HG_PREP_ROWS, prep_body, 0)

    def chunk(cidx, d, state_t):
        rows = pl.ds(pl.multiple_of(cidx * c, c), c)
        v = v_s[rows, :]
        scores = lax.dot_general(qe_s[d, rows, :], ke_s[d, rows, :], nt, preferred_element_type=F32)
        scores = jnp.where(causal if d == 0 else anti, scores, 0.0)
        o = jnp.dot(scores.astype(BF16), v, preferred_element_type=F32)
        o = o + lax.dot_general(qb_s[d, rows, :], state_t.astype(BF16), nt, preferred_element_type=F32)
        kv_t = lax.dot_general(v, kd_s[d, rows, :], tn, preferred_element_type=F32)
        (of_s if d == 0 else ob_s)[rows, :] = o
        return state_t * dec_s[d, cidx] + kv_t

    def body(i, carry):
        sf, sb = carry
        sf = chunk(i, 0, sf)
        jb = jnp.where(i < n_ctx_chunks, n_ctx_chunks - 1 - i, n_chunks - 1 + n_ctx_chunks - i)
        sb = chunk(jb, 1, sb)
        return sf, sb

    zero = jnp.zeros((HG_DK, HG_DK), F32)
    lax.fori_loop(0, n_chunks, body, (zero, zero), unroll=HG_UNROLL)

    o = of_s[CTX_LEN:, :] + ob_s[CTX_LEN:, :]
    o = o * lax.rsqrt(jnp.mean(o * o, axis=-1, keepdims=True) + EPS)
    o_ref[...] = (o * gn_ref[0] * jax.nn.silu(og_ref[...])).astype(o_ref.dtype)


def _hgrn_branch(z, lb_logits, hg_norm, layer):
    n_slots = lb_logits.shape[1]
    ctx_blk0 = N_LAT_ROWS // CTX_LEN

    def windows(i):
        col_blk = (D_A + i * D_B) // HG_DK
        return [pl.BlockSpec((CTX_LEN, HG_DK), lambda b, h: (ctx_blk0 + b, col_blk + h)),
                pl.BlockSpec((SEQ, HG_DK), lambda b, h: (b, col_blk + h))]

    in_specs = [spec for i in range(4) for spec in windows(i)]
    in_specs += [windows(4)[1],
                 pl.BlockSpec((2, n_slots, 1, 1, HG_DK), lambda b, h: (0, 0, h, 0, 0)),
                 pl.BlockSpec((1, 1, HG_DK), lambda b, h: (h, 0, 0))]
    operand = lambda: pltpu.VMEM((2, T_ALL, HG_DK), BF16)
    lg = lb_logits.astype(F32).reshape(2, n_slots, HG_HEADS, 1, HG_DK)
    return pl.pallas_call(
        functools.partial(_hgrn_kernel, layer=layer),
        grid=(BATCH, HG_HEADS),
        in_specs=in_specs,
        out_specs=pl.BlockSpec((SEQ, HG_DK), lambda b, h: (b, h)),
        out_shape=jax.ShapeDtypeStruct((N_LAT_ROWS, D_B), BF16),
        scratch_shapes=[operand(), operand(), operand(), operand(),
                        pltpu.VMEM((T_ALL, HG_DK), BF16),
                        pltpu.VMEM((2, T_ALL // HG_CHUNK, 1, HG_DK), F32),
                        pltpu.VMEM((T_ALL, HG_DK), F32),
                        pltpu.VMEM((T_ALL, HG_DK), F32)],
        compiler_params=_params(("parallel", "parallel"), 40),
        name="hgrn2",
    )(*([z] * 9), lg, hg_norm.reshape(HG_HEADS, 1, HG_DK))


FF_DOWN_TILE = 512


def kernel(x, c, ctx, c_ctx, w_ada, b_ada, norm_ffn1, w1_ffn1, w3_ffn1, w2_ffn1, norm_mix, w_in,
           s5_lam_re, s5_lam_im, s5_log_dt, s5_b_re, s5_b_im, s5_c_re, s5_c_im, s5_d, s5_w_glu,
           hg_lb_logits, hg_norm, w_proj_a, w_proj_b, w_out, norm_ffn2, w1_ffn2, w3_ffn2, w2_ffn2,
           norm_final):
    layer = 0
    c8 = jnp.concatenate([c, c_ctx[None], jnp.zeros((8 - BATCH - 1, D_MODEL), F32)], axis=0)
    mod = _ada(c8, w_ada[layer], b_ada[layer]).reshape(8, N_MOD, D_MODEL)
    mod = jnp.concatenate([mod[BATCH:BATCH + 1], mod[:BATCH]], axis=0)
    m = [mod[:, i].reshape(BATCH + 1, 1, D_MODEL) for i in range(N_MOD)]

    tile = 1024

    z1, h0 = _entry(x, ctx, norm_ffn1[layer], m[0], m[1])
    g1, w2_bf16 = _swiglu_up(z1, w1_ffn1[layer], w3_ffn1[layer], w2_ffn1[layer])
    h1 = _mm_res(g1, w2_bf16, h0, 0, m[2], 0.5, _group_full(FF_DOWN_TILE),
                 FF_DOWN_TILE, FF_DOWN_TILE, D_FF, 56)

    z2 = _norm_mod(h1, norm_mix[layer], m[3], m[4], _group_full(512))
    z = _mm(z2, w_in[layer], F32, tile, 512, 52, "mixer_in_proj")

    y_a = _s5_branch(z, s5_lam_re[layer], s5_lam_im[layer], s5_log_dt[layer], s5_b_re[layer],
                     s5_b_im[layer], s5_c_re[layer], s5_c_im[layer], s5_d[layer], s5_w_glu[layer])
    y_b = _hgrn_branch(z, hg_lb_logits, hg_norm[layer], layer)

    ga_tile = (D_A + 5 * D_B) // 512
    merged = _merge(y_a, w_proj_a[layer], y_b, w_proj_b[layer], z,
                    ga_tile, ga_tile + D_MODEL // 512, 0)
    h2 = _mm_res(merged, w_out[layer], h1, 0, m[5], 1.0, _group_lat(tile),
                 tile, 512, D_MODEL, 52)

    z3 = _norm_mod(h2, norm_ffn2[layer], m[6], m[7], _group_lat(512))
    g2, w2_bf16 = _swiglu_up(z3, w1_ffn2[layer], w3_ffn2[layer], w2_ffn2[layer])
    h3 = _mm_res(g2, w2_bf16, h2, 0, m[8], 0.5, _group_lat(FF_DOWN_TILE),
                 FF_DOWN_TILE, FF_DOWN_TILE, D_FF, 56)
    out = _final_norm(h3, norm_final).reshape(BATCH, GRID_W, GRID_ROWS, D_MODEL)
    return out.swapaxes(1, 2).reshape(BATCH, SEQ, D_MODEL)
```

```python
import functools
import math

import jax
import jax.numpy as jnp
from jax import lax
from jax.experimental import pallas as pl
from jax.experimental.pallas import tpu as pltpu

F32 = jnp.float32
BF16 = jnp.bfloat16

D_MODEL = 4096
BATCH = 4
SEQ = 2048
GRID_W = 64
GRID_ROWS = SEQ // GRID_W
CTX_LEN = 256
T_ALL = CTX_LEN + SEQ
D_FF = 11008
D_A = 1024
S5_H = 16
S5_G = D_A // S5_H
S5_P = 64
S5_L = 16
D_B = 2048
HG_DK = 128
HG_HEADS = D_B // HG_DK
HG_CHUNK = 64
N_MOD = 9
N_IN = D_A + 5 * D_B + 2 * D_MODEL
EPS = 1e-6

N_CTX_ROWS = BATCH * CTX_LEN
N_LAT_ROWS = BATCH * SEQ
N_ROWS = N_CTX_ROWS + N_LAT_ROWS
MIB = 1024 * 1024


def _params(semantics, vmem_mib):
    return pltpu.CompilerParams(dimension_semantics=semantics, vmem_limit_bytes=vmem_mib * MIB)


def _group_full(tm):
    n_lat_tiles, per_batch = N_LAT_ROWS // tm, SEQ // tm
    return lambda i: jnp.where(i < n_lat_tiles, 1 + i // per_batch, 0)


def _group_lat(tm):
    per_batch = SEQ // tm
    return lambda i: 1 + i // per_batch


def _ada_kernel(c_ref, w_ref, b_ref, o_ref):
    sc = jax.nn.silu(c_ref[...]).astype(BF16)
    o_ref[...] = jnp.dot(sc, w_ref[...].astype(BF16), preferred_element_type=F32) + b_ref[...]


def _ada(c8, w_ada, b_ada):
    n = w_ada.shape[1]
    tn = 1024
    return pl.pallas_call(
        _ada_kernel,
        grid=(n // tn,),
        in_specs=[pl.BlockSpec((8, D_MODEL), lambda j: (0, 0)),
                  pl.BlockSpec((D_MODEL, tn), lambda j: (0, j)),
                  pl.BlockSpec((1, tn), lambda j: (0, j))],
        out_specs=pl.BlockSpec((8, tn), lambda j: (0, j)),
        out_shape=jax.ShapeDtypeStruct((8, n), F32),
        compiler_params=_params(("parallel",), 48),
        name="ada_table",
    )(c8, w_ada, b_ada.reshape(1, n))


def _norm_mod_kernel(x_ref, g_ref, sh_ref, sc_ref, o_ref):
    x = x_ref[...]
    var = jnp.mean(x * x, axis=-1, keepdims=True)
    y = x * lax.rsqrt(var + EPS) * g_ref[...]
    o_ref[...] = (y * (1.0 + sc_ref[0]) + sh_ref[0]).astype(o_ref.dtype)


def _norm_mod(x, gain, shift, scale, group_fn, tm=512):
    rows = x.shape[0]
    vec = pl.BlockSpec((1, 1, D_MODEL), lambda i: (group_fn(i), 0, 0))
    return pl.pallas_call(
        _norm_mod_kernel,
        grid=(rows // tm,),
        in_specs=[pl.BlockSpec((tm, D_MODEL), lambda i: (i, 0)),
                  pl.BlockSpec((1, D_MODEL), lambda i: (0, 0)),
                  vec, vec],
        out_specs=pl.BlockSpec((tm, D_MODEL), lambda i: (i, 0)),
        out_shape=jax.ShapeDtypeStruct((rows, D_MODEL), BF16),
        compiler_params=_params(("parallel",), 40),
        name="norm_modulate",
    )(x, gain.reshape(1, D_MODEL), shift, scale)


ENTRY_COLS = 8


def _entry_kernel(x_ref, c_ref, g_ref, sh_ref, sc_ref, z_ref, h_ref, *, n_lat_tiles):
    def emit(rows, x):
        var = jnp.mean(x * x, axis=-1, keepdims=True)
        y = x * lax.rsqrt(var + EPS) * g_ref[...]
        z_ref[rows, :] = (y * (1.0 + sc_ref[0]) + sh_ref[0]).astype(z_ref.dtype)
        h_ref[rows, :] = x

    @pl.when(pl.program_id(0) < n_lat_tiles)
    def _():
        for col in range(ENTRY_COLS):
            emit(slice(col * GRID_ROWS, (col + 1) * GRID_ROWS),
                 x_ref[:, col * D_MODEL:(col + 1) * D_MODEL])

    @pl.when(pl.program_id(0) >= n_lat_tiles)
    def _():
        emit(slice(None), c_ref[...])


def _entry(x, ctx, gain, shift, scale):
    tm = ENTRY_COLS * GRID_ROWS
    tiles_per_batch = GRID_W // ENTRY_COLS
    n_lat_tiles = BATCH * tiles_per_batch
    lat = lambda i: jnp.minimum(i, n_lat_tiles - 1)
    group = lambda i: jnp.where(i < n_lat_tiles, 1 + i // tiles_per_batch, 0)
    vec = pl.BlockSpec((1, 1, D_MODEL), lambda i: (group(i), 0, 0))
    rows = pl.BlockSpec((tm, D_MODEL), lambda i: (i, 0))
    x_cols = x.reshape(BATCH, GRID_ROWS, GRID_W * D_MODEL)
    return pl.pallas_call(
        functools.partial(_entry_kernel, n_lat_tiles=n_lat_tiles),
        grid=(N_ROWS // tm,),
        in_specs=[pl.BlockSpec((None, GRID_ROWS, ENTRY_COLS * D_MODEL),
                               lambda i: (lat(i) // tiles_per_batch, 0, lat(i) % tiles_per_batch)),
                  pl.BlockSpec((tm, D_MODEL), lambda i: (jnp.maximum(i - n_lat_tiles, 0), 0)),
                  pl.BlockSpec((1, D_MODEL), lambda i: (0, 0)),
                  vec, vec],
        out_specs=[rows, rows],
        out_shape=[jax.ShapeDtypeStruct((N_ROWS, D_MODEL), BF16),
                   jax.ShapeDtypeStruct((N_ROWS, D_MODEL), F32)],
        compiler_params=_params(("parallel",), 40),
        name="entry_norm_modulate",
    )(x_cols, ctx.reshape(N_CTX_ROWS, D_MODEL), gain.reshape(1, D_MODEL), shift, scale)


def _final_norm_kernel(x_ref, g_ref, o_ref):
    for col in range(ENTRY_COLS):
        x = x_ref[col * GRID_ROWS:(col + 1) * GRID_ROWS, :]
        var = jnp.mean(x * x, axis=-1, keepdims=True)
        o_ref[:, col * D_MODEL:(col + 1) * D_MODEL] = x * lax.rsqrt(var + EPS) * g_ref[...]


def _final_norm(x, gain):
    tm = ENTRY_COLS * GRID_ROWS
    tiles_per_batch = GRID_W // ENTRY_COLS
    return pl.pallas_call(
        _final_norm_kernel,
        grid=(N_LAT_ROWS // tm,),
        in_specs=[pl.BlockSpec((tm, D_MODEL), lambda i: (i, 0)),
                  pl.BlockSpec((1, D_MODEL), lambda i: (0, 0))],
        out_specs=pl.BlockSpec((None, GRID_ROWS, ENTRY_COLS * D_MODEL),
                               lambda i: (i // tiles_per_batch, 0, i % tiles_per_batch)),
        out_shape=jax.ShapeDtypeStruct((BATCH, GRID_ROWS, GRID_W * D_MODEL), F32),
        compiler_params=_params(("parallel",), 40),
        name="final_norm",
    )(x, gain.reshape(1, D_MODEL))


W2_CAST_ROWS = 64


def _swiglu_up_kernel(z_ref, w1_ref, w3_ref, w2_ref, o_ref, w2_bf16_ref):
    z = z_ref[...]
    h1 = jnp.dot(z, w1_ref[...].astype(BF16), preferred_element_type=F32)
    h3 = jnp.dot(z, w3_ref[...].astype(BF16), preferred_element_type=F32)
    o_ref[...] = (jax.nn.silu(h1) * h3).astype(o_ref.dtype)
    w2_bf16_ref[...] = w2_ref[...].astype(BF16)


def _swiglu_up(z, w1, w3, w2, tm=1024, tn=512):
    m, k = z.shape
    n = w1.shape[1]
    nj = pl.cdiv(n, tn)
    n_w2_blocks = w2.shape[0] // W2_CAST_ROWS
    assert (m // tm) * nj >= n_w2_blocks and w2.shape[0] % W2_CAST_ROWS == 0
    w2_spec = pl.BlockSpec((W2_CAST_ROWS, w2.shape[1]),
                           lambda i, j: (jnp.minimum(i * nj + j, n_w2_blocks - 1), 0))
    return pl.pallas_call(
        _swiglu_up_kernel,
        grid=(m // tm, nj),
        in_specs=[pl.BlockSpec((tm, k), lambda i, j: (i, 0), pipeline_mode=pl.Buffered(1)),
                  pl.BlockSpec((k, tn), lambda i, j: (0, j)),
                  pl.BlockSpec((k, tn), lambda i, j: (0, j)),
                  w2_spec],
        out_specs=[pl.BlockSpec((tm, tn), lambda i, j: (i, j)), w2_spec],
        out_shape=[jax.ShapeDtypeStruct((m, n), BF16), jax.ShapeDtypeStruct(w2.shape, BF16)],
        compiler_params=_params(("arbitrary", "arbitrary"), 56),
        name="swiglu_up",
    )(z, w1, w3, w2)


def _mm_res_kernel(a_ref, w_ref, res_ref, gate_ref, o_ref, *scratch, scale, nk):
    if nk == 1:
        prod = jnp.dot(a_ref[...], w_ref[...].astype(BF16), preferred_element_type=F32)
        o_ref[...] = res_ref[...] + (scale * gate_ref[0]) * prod
        return
    acc_ref, = scratch
    k = pl.program_id(2)

    @pl.when(k == 0)
    def _():
        acc_ref[...] = jnp.zeros_like(acc_ref)

    acc_ref[...] += jnp.dot(a_ref[...], w_ref[...].astype(BF16), preferred_element_type=F32)

    @pl.when(k == nk - 1)
    def _():
        o_ref[...] = res_ref[...] + (scale * gate_ref[0]) * acc_ref[...]


def _mm_res(a, w, res, res_tile_off, gate, scale, group_fn, tm, tn, tk, vmem_mib):
    m, k = a.shape
    n = w.shape[1]
    nk = k // tk
    return pl.pallas_call(
        functools.partial(_mm_res_kernel, scale=scale, nk=nk),
        grid=(m // tm, n // tn, nk),
        in_specs=[pl.BlockSpec((tm, tk), lambda i, j, kk: (i, kk)),
                  pl.BlockSpec((tk, tn), lambda i, j, kk: (kk, j)),
                  pl.BlockSpec((tm, tn), lambda i, j, kk: (i + res_tile_off, j)),
                  pl.BlockSpec((1, 1, tn), lambda i, j, kk: (group_fn(i), 0, j))],
        out_specs=pl.BlockSpec((tm, tn), lambda i, j, kk: (i, j)),
        out_shape=jax.ShapeDtypeStruct((m, n), F32),
        scratch_shapes=[] if nk == 1 else [pltpu.VMEM((tm, tn), F32)],
        compiler_params=_params(("parallel", "parallel", "arbitrary"), vmem_mib),
        name="matmul_gated_residual",
    )(a, w, res, gate)


def _mm_kernel(a_ref, w_ref, o_ref):
    o_ref[...] = jnp.dot(a_ref[...], w_ref[...].astype(BF16), preferred_element_type=F32).astype(o_ref.dtype)


def _mm(a, w, out_dtype, tm, tn, vmem_mib, name):
    m, k = a.shape
    n = w.shape[1]
    return pl.pallas_call(
        _mm_kernel,
        grid=(m // tm, n // tn),
        in_specs=[pl.BlockSpec((tm, k), lambda i, j: (i, 0)),
                  pl.BlockSpec((k, tn), lambda i, j: (0, j))],
        out_specs=pl.BlockSpec((tm, tn), lambda i, j: (i, j)),
        out_shape=jax.ShapeDtypeStruct((m, n), out_dtype),
        compiler_params=_params(("parallel", "arbitrary"), vmem_mib),
        name=name,
    )(a, w)


def _merge_kernel(ya_ref, wa_ref, yb_ref, wb_ref, ga_ref, gb_ref, o_ref):
    pa = jnp.dot(ya_ref[...], wa_ref[...].astype(BF16), preferred_element_type=F32)
    pb = jnp.dot(yb_ref[...], wb_ref[...].astype(BF16), preferred_element_type=F32)
    o_ref[...] = (jax.nn.sigmoid(ga_ref[...]) * pa + jax.nn.sigmoid(gb_ref[...]) * pb).astype(o_ref.dtype)


def _merge(ya, wa, yb, wb, z, ga_col_tile, gb_col_tile, z_row_tile_off, tm=1024, tn=512):
    m = ya.shape[0]
    n = wa.shape[1]
    return pl.pallas_call(
        _merge_kernel,
        grid=(m // tm, n // tn),
        in_specs=[pl.BlockSpec((tm, D_A), lambda i, j: (i, 0)),
                  pl.BlockSpec((D_A, tn), lambda i, j: (0, j)),
                  pl.BlockSpec((tm, D_B), lambda i, j: (i, 0)),
                  pl.BlockSpec((D_B, tn), lambda i, j: (0, j)),
                  pl.BlockSpec((tm, tn), lambda i, j: (i + z_row_tile_off, ga_col_tile + j)),
                  pl.BlockSpec((tm, tn), lambda i, j: (i + z_row_tile_off, gb_col_tile + j))],
        out_specs=pl.BlockSpec((tm, tn), lambda i, j: (i, j)),
        out_shape=jax.ShapeDtypeStruct((m, n), BF16),
        compiler_params=_params(("parallel", "arbitrary"), 48),
        name="gated_merge",
    )(ya, wa, yb, wb, z, z)


S5_PRECISION = lax.Precision.HIGHEST


def _cmul(a, b):
    return a[0] * b[0] - a[1] * b[1], a[0] * b[1] + a[1] * b[0]


def _s5_prep_kernel(ldt_ref, lam_row_ref, lam_col_ref, b1_ref, b2_ref, c1_ref, c2_ref, dt_ref,
                    wst_ref, wc_ref, a16_ref):
    g = pl.program_id(0)
    lh = S5_L * S5_H
    lane_sign = jnp.where(lax.broadcasted_iota(jnp.int32, (1, 2 * S5_P), 1) < S5_P, -1.0, 1.0)
    row_sign = jnp.where(lax.broadcasted_iota(jnp.int32, (2 * S5_P, 1), 0) < S5_P, 1.0, -1.0)
    log2_h = S5_H.bit_length() - 1
    log2_l = S5_L.bit_length() - 1
    t_idx = lax.shift_right_logical(lax.broadcasted_iota(jnp.int32, (1, lh), 1), log2_h)
    b1 = b1_ref[0]
    b2 = b2_ref[0]
    c1 = c1_ref[0]
    c2 = c2_ref[0]
    toep = None
    for d in range(2):
        dt = jnp.exp(jnp.full((1, 1), ldt_ref[d, g], F32))
        lr = lam_row_ref[d, 0]
        li = lam_row_ref[2 + d, 0]
        mag = jnp.exp(lr * dt)
        ab_re, ab_im = mag * jnp.cos(li * dt), mag * jnp.sin(li * dt)
        den = lr * lr + li * li
        nr = ab_re - 1.0
        kr = (nr * lr + ab_im * li) / den
        ki = (ab_im * lr - nr * li) / den
        bk = kr * b1 + lane_sign * ki * b2
        bk_sw = lane_sign * (kr * b2 - lane_sign * ki * b1)
        pw = [(jnp.ones_like(ab_re), jnp.zeros_like(ab_re))]
        for _ in range(S5_L):
            pw.append(_cmul(pw[-1], (ab_re, ab_im)))
        taus = range(S5_L - 1, -1, -1) if d == 0 else range(S5_L)
        rows_of = lambda part: jnp.concatenate(
            [jnp.broadcast_to(pw[tau][part], (S5_H, 2 * S5_P)) for tau in taus], axis=0)
        wst_ref[0, :, d * 2 * S5_P:(d + 1) * 2 * S5_P] = rows_of(0) * bk + rows_of(1) * bk_sw
        a16_ref[0, 2 * d:2 * d + 1, :] = pw[S5_L][0]
        a16_ref[0, 2 * d + 1:2 * d + 2, :] = lane_sign * pw[S5_L][1]
        lrc = lam_col_ref[d, 0]
        lic = lam_col_ref[2 + d, 0]
        magc = jnp.exp(lrc * dt)
        squares = [(magc * jnp.cos(lic * dt), magc * jnp.sin(lic * dt))]
        for _ in range(log2_l - 1):
            squares.append(_cmul(squares[-1], squares[-1]))
        tau_col = t_idx if d == 0 else (S5_L - 1) - t_idx
        q0 = None
        for j, sq in enumerate(squares):
            bit = (lax.shift_right_logical(tau_col, j) & 1) == 1
            factor = (jnp.where(bit, sq[0], 1.0), jnp.where(bit, sq[1], 0.0))
            q0 = factor if q0 is None else _cmul(q0, factor)
        q1 = _cmul(q0, squares[0])

        def c_times(q):
            return row_sign * c1 * q[0] - c2 * q[1]

        wc_ref[0, d * 2 * S5_P:(d + 1) * 2 * S5_P, :] = c_times(q1)
        m_all = jnp.dot(bk[0:S5_H], c_times(q0), precision=S5_PRECISION,
                        preferred_element_type=F32)
        lane = lax.broadcasted_iota(jnp.int32, (S5_H, lh), 1)
        blocks = []
        for s in range(S5_L):
            if d == 0:
                shift, keep = S5_H * s, lane >= S5_H * s
            else:
                shift, keep = (S5_H * (s + 1)) % lh, lane < S5_H * (s + 1)
            rolled = m_all if shift == 0 else pltpu.roll(m_all, shift, axis=1)
            blocks.append(jnp.where(keep, rolled, 0.0))
        td = jnp.concatenate(blocks, axis=0)
        toep = td if toep is None else toep + td
    diag = lax.broadcasted_iota(jnp.int32, (lh, lh), 0) == lax.broadcasted_iota(jnp.int32, (lh, lh), 1)
    wc_ref[0, 4 * S5_P:, :] = toep + jnp.where(diag, dt_ref[0], 0.0)


def _s5_prep(lam_re, lam_im, log_dt, b_re, b_im, c_re, c_im, d_skip):
    lh = S5_L * S5_H
    dup = lambda z: jnp.concatenate([z, z], axis=-1)
    lam = jnp.concatenate([lam_re, lam_im], axis=0)
    lam_row = dup(lam)[:, :, None, :]
    lam_col = dup(lam)[:, :, :, None]
    bt_re, bt_im = jnp.swapaxes(b_re, 1, 2), jnp.swapaxes(b_im, 1, 2)
    b1 = jnp.tile(jnp.concatenate([bt_re, bt_im], axis=-1), (1, S5_L, 1))
    b2 = jnp.tile(jnp.concatenate([bt_im, bt_re], axis=-1), (1, S5_L, 1))
    ct_re, ct_im = jnp.swapaxes(c_re, 1, 2), jnp.swapaxes(c_im, 1, 2)
    c1 = jnp.tile(jnp.concatenate([ct_re, ct_im], axis=1), (1, 1, S5_L))
    c2 = jnp.tile(jnp.concatenate([ct_im, ct_re], axis=1), (1, 1, S5_L))
    d_tiled = jnp.tile(d_skip.reshape(S5_G, 1, S5_H), (1, 1, S5_L))
    grp = lambda shape: pl.BlockSpec((1,) + shape, lambda g: (g, 0, 0))
    return pl.pallas_call(
        _s5_prep_kernel,
        grid=(S5_G,),
        in_specs=[pl.BlockSpec(memory_space=pltpu.SMEM),
                  pl.BlockSpec((4, 1, 1, 2 * S5_P), lambda g: (0, g, 0, 0)),
                  pl.BlockSpec((4, 1, 2 * S5_P, 1), lambda g: (0, g, 0, 0)),
                  grp((lh, 2 * S5_P)), grp((lh, 2 * S5_P)),
                  grp((2 * S5_P, lh)), grp((2 * S5_P, lh)),
                  grp((1, lh))],
        out_specs=[grp((lh, 4 * S5_P)), grp((4 * S5_P + lh, lh)), grp((4, 2 * S5_P))],
        out_shape=[jax.ShapeDtypeStruct((S5_G, lh, 4 * S5_P), F32),
                   jax.ShapeDtypeStruct((S5_G, 4 * S5_P + lh, lh), F32),
                   jax.ShapeDtypeStruct((S5_G, 4, 2 * S5_P), F32)],
        compiler_params=_params(("parallel",), 32),
        name="s5_prep",
    )(log_dt, lam_row, lam_col, b1, b2, c1, c2, d_tiled)


def _s5_local_kernel(ul_ref, uc_ref, wst_ref, xf_ref, xb_ref):
    n_lat = ul_ref.shape[1]
    dot = functools.partial(jnp.dot, precision=S5_PRECISION, preferred_element_type=F32)
    r_lat = dot(ul_ref[0], wst_ref[0])
    r_ctx = dot(uc_ref[0], wst_ref[0])
    xf_ref[0:n_lat, :] = r_lat[:, :2 * S5_P]
    xf_ref[n_lat:, :] = r_ctx[:, :2 * S5_P]
    xb_ref[0:n_lat, :] = r_lat[:, 2 * S5_P:]
    xb_ref[n_lat:, :] = r_ctx[:, 2 * S5_P:]


def _s5_local(ug_lat, ug_ctx, wst):
    n_lat, n_ctx = ug_lat.shape[1], ug_ctx.shape[1]
    lh = S5_L * S5_H
    col = pl.BlockSpec((n_lat + n_ctx, 2 * S5_P), lambda g: (0, g))
    shape = jax.ShapeDtypeStruct((n_lat + n_ctx, S5_G * 2 * S5_P), F32)
    return pl.pallas_call(
        _s5_local_kernel,
        grid=(S5_G,),
        in_specs=[pl.BlockSpec((1, n_lat, lh), lambda g: (g, 0, 0)),
                  pl.BlockSpec((1, n_ctx, lh), lambda g: (g, 0, 0)),
                  pl.BlockSpec((1, lh, 4 * S5_P), lambda g: (g, 0, 0))],
        out_specs=[col, col],
        out_shape=[shape, shape],
        compiler_params=_params(("parallel",), 32),
        name="s5_local_state",
    )(ug_lat, ug_ctx, wst)


S5_SCAN_BLOCK = 8


def _s5_scan_kernel(xf_ref, xb_ref, a_ref, pf_ref, pb_ref, sf_ref, sb_ref):
    @pl.when(pl.program_id(0) == 0)
    def _():
        sf_ref[...] = jnp.zeros_like(sf_ref)
        sb_ref[...] = jnp.zeros_like(sb_ref)

    lanes = sf_ref.shape[1]
    in_re_half = (lax.broadcasted_iota(jnp.int32, (1, lanes), 1) & (2 * S5_P - 1)) < S5_P

    def step(s, x, aa, ab):
        partner = jnp.where(in_re_half, pltpu.roll(s, lanes - S5_P, axis=1), pltpu.roll(s, S5_P, axis=1))
        return aa * s + ab * partner + x

    s = sf_ref[...]
    for j in range(S5_SCAN_BLOCK):
        rows = slice(j * BATCH, (j + 1) * BATCH)
        pf_ref[rows, :] = s
        s = step(s, xf_ref[rows, :], a_ref[0], a_ref[1])
    sf_ref[...] = s
    s = sb_ref[...]
    for j in reversed(range(S5_SCAN_BLOCK)):
        rows = slice(j * BATCH, (j + 1) * BATCH)
        pb_ref[rows, :] = s
        s = step(s, xb_ref[rows, :], a_ref[2], a_ref[3])
    sb_ref[...] = s


def _s5_scan(xf, xb, a16):
    rows, lanes = xf.shape
    blk_rows = S5_SCAN_BLOCK * BATCH
    nblk = rows // blk_rows
    ctx_blk = (CTX_LEN // S5_L) // S5_SCAN_BLOCK
    lat_blk = nblk - ctx_blk
    fwd = lambda i: (jnp.where(i < ctx_blk, lat_blk + i, i - ctx_blk), 0)
    bwd = lambda i: (nblk - 1 - i, 0)
    blk = (blk_rows, lanes)
    shape = jax.ShapeDtypeStruct(xf.shape, F32)
    return pl.pallas_call(
        _s5_scan_kernel,
        grid=(nblk,),
        in_specs=[pl.BlockSpec(blk, fwd), pl.BlockSpec(blk, bwd),
                  pl.BlockSpec((4, 1, lanes), lambda i: (0, 0, 0))],
        out_specs=[pl.BlockSpec(blk, fwd), pl.BlockSpec(blk, bwd)],
        out_shape=[shape, shape],
        scratch_shapes=[pltpu.VMEM((BATCH, lanes), F32), pltpu.VMEM((BATCH, lanes), F32)],
        compiler_params=_params(("arbitrary",), 32),
        name="s5_chunk_scan",
    )(xf, xb, a16)


def _s5_out_kernel(pf_ref, pb_ref, u_ref, wc_ref, y_ref):
    dot = functools.partial(jnp.dot, precision=S5_PRECISION, preferred_element_type=F32)
    y = (dot(pf_ref[...], wc_ref[0, 0:2 * S5_P, :])
         + dot(pb_ref[...], wc_ref[0, 2 * S5_P:4 * S5_P, :])
         + dot(u_ref[0], wc_ref[0, 4 * S5_P:, :]))
    y_ref[0] = jax.nn.gelu(y)


def _s5_out(pf, pb, ug_lat, wc):
    rows = ug_lat.shape[1]
    lh = S5_L * S5_H
    col = pl.BlockSpec((rows, 2 * S5_P), lambda g: (0, g))
    return pl.pallas_call(
        _s5_out_kernel,
        grid=(S5_G,),
        in_specs=[col, col,
                  pl.BlockSpec((1, rows, lh), lambda g: (g, 0, 0)),
                  pl.BlockSpec((1, 4 * S5_P + lh, lh), lambda g: (g, 0, 0))],
        out_specs=pl.BlockSpec((1, rows, lh), lambda g: (g, 0, 0)),
        out_shape=jax.ShapeDtypeStruct((S5_G, rows, lh), F32),
        compiler_params=_params(("parallel",), 32),
        name="s5_output",
    )(pf, pb, ug_lat, wc)


def _s5_glu_kernel(y_ref, w_ref, o_ref):
    y = y_ref[...]
    gate = jnp.dot(y.astype(BF16), w_ref[...].astype(BF16), preferred_element_type=F32)
    o_ref[...] = (y * jax.nn.sigmoid(gate)).astype(o_ref.dtype)


def _s5_glu(y, w_glu, tm=1024):
    m = y.shape[0]
    return pl.pallas_call(
        _s5_glu_kernel,
        grid=(m // tm,),
        in_specs=[pl.BlockSpec((tm, D_A), lambda i: (i, 0)),
                  pl.BlockSpec((D_A, D_A), lambda i: (0, 0))],
        out_specs=pl.BlockSpec((tm, D_A), lambda i: (i, 0)),
        out_shape=jax.ShapeDtypeStruct((m, D_A), BF16),
        compiler_params=_params(("parallel",), 32),
        name="s5_glu",
    )(y, w_glu)


def _s5_branch(z, lam_re, lam_im, log_dt, b_re, b_im, c_re, c_im, d_skip, w_glu):
    wst, wc, a16 = _s5_prep(lam_re, lam_im, log_dt, b_re, b_im, c_re, c_im, d_skip)
    nc_ctx, nc_lat = CTX_LEN // S5_L, SEQ // S5_L
    n_chunks = nc_ctx + nc_lat
    u = z[:, :D_A]
    ug_ctx = u[N_LAT_ROWS:].reshape(BATCH, nc_ctx, S5_L, S5_G, S5_H).transpose(3, 1, 0, 2, 4)
    ug_ctx = ug_ctx.reshape(S5_G, nc_ctx * BATCH, S5_L * S5_H)
    ug_lat = u[:N_LAT_ROWS].reshape(BATCH, GRID_W // S5_L, S5_L, GRID_ROWS, S5_G, S5_H)
    ug_lat = ug_lat.transpose(4, 3, 1, 0, 2, 5).reshape(S5_G, nc_lat * BATCH, S5_L * S5_H)
    xf, xb = _s5_local(ug_lat, ug_ctx, wst)
    a16_lanes = jnp.swapaxes(a16, 0, 1).reshape(4, 1, S5_G * 2 * S5_P)
    pf, pb = _s5_scan(xf, xb, a16_lanes)
    yg = _s5_out(pf, pb, ug_lat, wc)
    y_lat = yg.reshape(S5_G, GRID_ROWS, GRID_W // S5_L, BATCH, S5_L, S5_H)
    y_lat = y_lat.transpose(3, 2, 4, 1, 0, 5).reshape(N_LAT_ROWS, D_A)
    return _s5_glu(y_lat, w_glu)


HG_PREP_ROWS = 256
HG_UNROLL = 12


def _chunk_cumsum(x, reverse):
    n = x.shape[0]
    pos = lax.broadcasted_iota(jnp.int32, (n, 1), 0) & (HG_CHUNK - 1)
    s = 1
    while s < HG_CHUNK:
        if reverse:
            x = x + jnp.where(pos < HG_CHUNK - s, pltpu.roll(x, n - s, axis=0), 0.0)
        else:
            x = x + jnp.where(pos >= s, pltpu.roll(x, s, axis=0), 0.0)
        s *= 2
    return x


def _hgrn_kernel(qc_ref, ql_ref, vc_ref, vl_ref, ffc_ref, ffl_ref, fbc_ref, fbl_ref, og_ref, lg_ref,
                 gn_ref, o_ref, qe_s, ke_s, qb_s, kd_s, v_s, dec_s, of_s, ob_s, *, layer):
    c = HG_CHUNK
    n_chunks = T_ALL // c
    n_ctx_chunks = CTX_LEN // c
    row = lax.broadcasted_iota(jnp.int32, (c, c), 0)
    col = lax.broadcasted_iota(jnp.int32, (c, c), 1)
    causal = row >= col
    anti = row <= col
    nt = (((1,), (1,)), ((), ()))
    tn = (((0,), (0,)), ((), ()))

    def lower_bound(d):
        logits = [lg_ref[d, j, 0] for j in range(lg_ref.shape[1])]
        top = functools.reduce(jnp.maximum, logits)
        e = [jnp.exp(l - top) for l in logits]
        return sum(e[:layer + 1]) / sum(e)

    lb = (lower_bound(0), lower_bound(1))

    def prepare(row0, chunk0, q, v, fpre):
        n = q.shape[0]
        rows = pl.ds(row0, n)
        as_chunks = lambda a: a.reshape(n // c, c, HG_DK)
        flat_bf16 = lambda a: a.reshape(n, HG_DK).astype(BF16)
        v_s[rows, :] = v.astype(BF16)
        q3 = as_chunks(q)
        for d in range(2):
            f = lb[d] + (1.0 - lb[d]) * jax.nn.sigmoid(fpre[d])
            k3 = as_chunks(1.0 - f)
            b3 = as_chunks(_chunk_cumsum(jnp.log(f), reverse=d == 1))
            mid = c // 2 - 1 if d == 0 else c // 2
            last = c - 1 if d == 0 else 0
            b_mid, b_last = b3[:, mid:mid + 1, :], b3[:, last:last + 1, :]
            qe_s[d, rows, :] = flat_bf16(q3 * jnp.exp(b3 - b_mid))
            ke_s[d, rows, :] = flat_bf16(k3 * jnp.exp(b_mid - b3))
            qb_s[d, rows, :] = flat_bf16(q3 * jnp.exp(b3))
            kd_s[d, rows, :] = flat_bf16(k3 * jnp.exp(b_last - b3))
            dec_s[d, pl.ds(chunk0, n // c)] = jnp.exp(b_last)

    prepare(0, 0, qc_ref[...], vc_ref[...], (ffc_ref[...], fbc_ref[...]))

    def prep_body(blk, carry):
        r0 = pl.multiple_of(blk * HG_PREP_ROWS, HG_PREP_ROWS)
        src = pl.ds(r0, HG_PREP_ROWS)
        prepare(pl.multiple_of(CTX_LEN + r0, HG_PREP_ROWS), n_ctx_chunks + blk * (HG_PREP_ROWS // c),
                ql_ref[src, :], vl_ref[src, :], (ffl_ref[src, :], fbl_ref[src, :]))
        return carry

    lax.fori_loop(0, SEQ // HG_PREP_ROWS, prep_body, 0)

    def chunk(cidx, d, state_t):
        rows = pl.ds(pl.multiple_of(cidx * c, c), c)
        v = v_s[rows, :]
        scores = lax.dot_general(qe_s[d, rows, :], ke_s[d, rows, :], nt, preferred_element_type=F32)
        scores = jnp.where(causal if d == 0 else anti, scores, 0.0)
        o = jnp.dot(scores.astype(BF16), v, preferred_element_type=F32)
        o = o + lax.dot_general(qb_s[d, rows, :], state_t.astype(BF16), nt, preferred_element_type=F32)
        kv_t = lax.dot_general(v, kd_s[d, rows, :], tn, preferred_element_type=F32)
        (of_s if d == 0 else ob_s)[rows, :] = o
        return state_t * dec_s[d, cidx] + kv_t

    def body(i, carry):
        sf, sb = carry
        sf = chunk(i, 0, sf)
        jb = jnp.where(i < n_ctx_chunks, n_ctx_chunks - 1 - i, n_chunks - 1 + n_ctx_chunks - i)
        sb = chunk(jb, 1, sb)
        return sf, sb

    zero = jnp.zeros((HG_DK, HG_DK), F32)
    lax.fori_loop(0, n_chunks, body, (zero, zero), unroll=HG_UNROLL)

    o = of_s[CTX_LEN:, :] + ob_s[CTX_LEN:, :]
    o = o * lax.rsqrt(jnp.mean(o * o, axis=-1, keepdims=True) + EPS)
    o_ref[...] = (o * gn_ref[0] * jax.nn.silu(og_ref[...])).astype(o_ref.dtype)


def _hgrn_branch(z, lb_logits, hg_norm, layer):
    n_slots = lb_logits.shape[1]
    ctx_blk0 = N_LAT_ROWS // CTX_LEN

    def windows(i):
        col_blk = (D_A + i * D_B) // HG_DK
        return [pl.BlockSpec((CTX_LEN, HG_DK), lambda b, h: (ctx_blk0 + b, col_blk + h)),
                pl.BlockSpec((SEQ, HG_DK), lambda b, h: (b, col_blk + h))]

    in_specs = [spec for i in range(4) for spec in windows(i)]
    in_specs += [windows(4)[1],
                 pl.BlockSpec((2, n_slots, 1, 1, HG_DK), lambda b, h: (0, 0, h, 0, 0)),
                 pl.BlockSpec((1, 1, HG_DK), lambda b, h: (h, 0, 0))]
    operand = lambda: pltpu.VMEM((2, T_ALL, HG_DK), BF16)
    lg = lb_logits.astype(F32).reshape(2, n_slots, HG_HEADS, 1, HG_DK)
    return pl.pallas_call(
        functools.partial(_hgrn_kernel, layer=layer),
        grid=(BATCH, HG_HEADS),
        in_specs=in_specs,
        out_specs=pl.BlockSpec((SEQ, HG_DK), lambda b, h: (b, h)),
        out_shape=jax.ShapeDtypeStruct((N_LAT_ROWS, D_B), BF16),
        scratch_shapes=[operand(), operand(), operand(), operand(),
                        pltpu.VMEM((T_ALL, HG_DK), BF16),
                        pltpu.VMEM((2, T_ALL // HG_CHUNK, 1, HG_DK), F32),
                        pltpu.VMEM((T_ALL, HG_DK), F32),
                        pltpu.VMEM((T_ALL, HG_DK), F32)],
        compiler_params=_params(("parallel", "parallel"), 40),
        name="hgrn2",
    )(*([z] * 9), lg, hg_norm.reshape(HG_HEADS, 1, HG_DK))


FF_DOWN_TILE = 512


def kernel(x, c, ctx, c_ctx, w_ada, b_ada, norm_ffn1, w1_ffn1, w3_ffn1, w2_ffn1, norm_mix, w_in,
           s5_lam_re, s5_lam_im, s5_log_dt, s5_b_re, s5_b_im, s5_c_re, s5_c_im, s5_d, s5_w_glu,
           hg_lb_logits, hg_norm, w_proj_a, w_proj_b, w_out, norm_ffn2, w1_ffn2, w3_ffn2, w2_ffn2,
           norm_final):
    layer = 0
    c8 = jnp.concatenate([c, c_ctx[None], jnp.zeros((8 - BATCH - 1, D_MODEL), F32)], axis=0)
    mod = _ada(c8, w_ada[layer], b_ada[layer]).reshape(8, N_MOD, D_MODEL)
    mod = jnp.concatenate([mod[BATCH:BATCH + 1], mod[:BATCH]], axis=0)
    m = [mod[:, i].reshape(BATCH + 1, 1, D_MODEL) for i in range(N_MOD)]

    tile = 1024

    z1, h0 = _entry(x, ctx, norm_ffn1[layer], m[0], m[1])
    g1, w2_bf16 = _swiglu_up(z1, w1_ffn1[layer], w3_ffn1[layer], w2_ffn1[layer])
    h1 = _mm_res(g1, w2_bf16, h0, 0, m[2], 0.5, _group_full(FF_DOWN_TILE),
                 FF_DOWN_TILE, FF_DOWN_TILE, D_FF, 56)

    z2 = _norm_mod(h1, norm_mix[layer], m[3], m[4], _group_full(512))
    z = _mm(z2, w_in[layer], F32, tile, 512, 52, "mixer_in_proj")

    y_a = _s5_branch(z, s5_lam_re[layer], s5_lam_im[layer], s5_log_dt[layer], s5_b_re[layer],
                     s5_b_im[layer], s5_c_re[layer], s5_c_im[layer], s5_d[layer], s5_w_glu[layer])
    y_b = _hgrn_branch(z, hg_lb_logits, hg_norm[layer], layer)

    ga_tile = (D_A + 5 * D_B) // 512
    merged = _merge(y_a, w_proj_a[layer], y_b, w_proj_b[layer], z,
                    ga_tile, ga_tile + D_MODEL // 512, 0)
    h2 = _mm_res(merged, w_out[layer], h1, 0, m[5], 1.0, _group_lat(tile),
                 tile, 512, D_MODEL, 52)

    z3 = _norm_mod(h2, norm_ffn2[layer], m[6], m[7], _group_lat(512))
    g2, w2_bf16 = _swiglu_up(z3, w1_ffn2[layer], w3_ffn2[layer], w2_ffn2[layer])
    h3 = _mm_res(g2, w2_bf16, h2, 0, m[8], 0.5, _group_lat(FF_DOWN_TILE),
                 FF_DOWN_TILE, FF_DOWN_TILE, D_FF, 56)
    return _final_norm(h3, norm_final).reshape(BATCH, SEQ, D_MODEL)
```

```python
import functools
import math

import jax
import jax.numpy as jnp
from jax import lax
from jax.experimental import pallas as pl
from jax.experimental.pallas import tpu as pltpu

F32 = jnp.float32
BF16 = jnp.bfloat16

D_MODEL = 4096
BATCH = 4
SEQ = 2048
GRID_W = 64
GRID_ROWS = SEQ // GRID_W
CTX_LEN = 256
T_ALL = CTX_LEN + SEQ
D_FF = 11008
D_A = 1024
S5_H = 16
S5_G = D_A // S5_H
S5_P = 64
S5_L = 16
D_B = 2048
HG_DK = 128
HG_HEADS = D_B // HG_DK
HG_CHUNK = 64
N_MOD = 9
N_IN = D_A + 5 * D_B + 2 * D_MODEL
EPS = 1e-6

N_CTX_ROWS = BATCH * CTX_LEN
N_LAT_ROWS = BATCH * SEQ
N_ROWS = N_CTX_ROWS + N_LAT_ROWS
MIB = 1024 * 1024


def _params(semantics, vmem_mib):
    return pltpu.CompilerParams(dimension_semantics=semantics, vmem_limit_bytes=vmem_mib * MIB)


def _group_full(tm):
    n_lat_tiles, per_batch = N_LAT_ROWS // tm, SEQ // tm
    return lambda i: jnp.where(i < n_lat_tiles, 1 + i // per_batch, 0)


def _group_lat(tm):
    per_batch = SEQ // tm
    return lambda i: 1 + i // per_batch


def _ada_kernel(c_ref, w_ref, b_ref, o_ref):
    sc = jax.nn.silu(c_ref[...]).astype(BF16)
    o_ref[...] = jnp.dot(sc, w_ref[...].astype(BF16), preferred_element_type=F32) + b_ref[...]


def _ada(c8, w_ada, b_ada):
    n = w_ada.shape[1]
    tn = 1024
    return pl.pallas_call(
        _ada_kernel,
        grid=(n // tn,),
        in_specs=[pl.BlockSpec((8, D_MODEL), lambda j: (0, 0)),
                  pl.BlockSpec((D_MODEL, tn), lambda j: (0, j)),
                  pl.BlockSpec((1, tn), lambda j: (0, j))],
        out_specs=pl.BlockSpec((8, tn), lambda j: (0, j)),
        out_shape=jax.ShapeDtypeStruct((8, n), F32),
        compiler_params=_params(("parallel",), 48),
        name="ada_table",
    )(c8, w_ada, b_ada.reshape(1, n))


def _norm_mod_kernel(x_ref, g_ref, sh_ref, sc_ref, o_ref):
    x = x_ref[...]
    var = jnp.mean(x * x, axis=-1, keepdims=True)
    y = x * lax.rsqrt(var + EPS) * g_ref[...]
    o_ref[...] = (y * (1.0 + sc_ref[0]) + sh_ref[0]).astype(o_ref.dtype)


def _norm_mod(x, gain, shift, scale, group_fn, tm=512):
    rows = x.shape[0]
    vec = pl.BlockSpec((1, 1, D_MODEL), lambda i: (group_fn(i), 0, 0))
    return pl.pallas_call(
        _norm_mod_kernel,
        grid=(rows // tm,),
        in_specs=[pl.BlockSpec((tm, D_MODEL), lambda i: (i, 0)),
                  pl.BlockSpec((1, D_MODEL), lambda i: (0, 0)),
                  vec, vec],
        out_specs=pl.BlockSpec((tm, D_MODEL), lambda i: (i, 0)),
        out_shape=jax.ShapeDtypeStruct((rows, D_MODEL), BF16),
        compiler_params=_params(("parallel",), 40),
        name="norm_modulate",
    )(x, gain.reshape(1, D_MODEL), shift, scale)


ENTRY_COLS = 8


def _entry_kernel(x_ref, c_ref, g_ref, sh_ref, sc_ref, z_ref, h_ref, *, n_lat_tiles):
    def emit(rows, x):
        var = jnp.mean(x * x, axis=-1, keepdims=True)
        y = x * lax.rsqrt(var + EPS) * g_ref[...]
        z_ref[rows, :] = (y * (1.0 + sc_ref[0]) + sh_ref[0]).astype(z_ref.dtype)
        h_ref[rows, :] = x

    @pl.when(pl.program_id(0) < n_lat_tiles)
    def _():
        x = pltpu.einshape("rcd->crd", x_ref[...])
        emit(slice(None), x.reshape(ENTRY_COLS * GRID_ROWS, D_MODEL))

    @pl.when(pl.program_id(0) >= n_lat_tiles)
    def _():
        emit(slice(None), c_ref[...])


def _entry(x, ctx, gain, shift, scale):
    tm = ENTRY_COLS * GRID_ROWS
    tiles_per_batch = GRID_W // ENTRY_COLS
    n_lat_tiles = BATCH * tiles_per_batch
    lat = lambda i: jnp.minimum(i, n_lat_tiles - 1)
    group = lambda i: jnp.where(i < n_lat_tiles, 1 + i // tiles_per_batch, 0)
    vec = pl.BlockSpec((1, 1, D_MODEL), lambda i: (group(i), 0, 0))
    rows = pl.BlockSpec((tm, D_MODEL), lambda i: (i, 0))
    x_cols = x.reshape(BATCH, GRID_ROWS, GRID_W, D_MODEL)
    return pl.pallas_call(
        functools.partial(_entry_kernel, n_lat_tiles=n_lat_tiles),
        grid=(N_ROWS // tm,),
        in_specs=[pl.BlockSpec((None, GRID_ROWS, ENTRY_COLS, D_MODEL),
                               lambda i: (lat(i) // tiles_per_batch, 0, lat(i) % tiles_per_batch, 0)),
                  pl.BlockSpec((tm, D_MODEL), lambda i: (jnp.maximum(i - n_lat_tiles, 0), 0)),
                  pl.BlockSpec((1, D_MODEL), lambda i: (0, 0)),
                  vec, vec],
        out_specs=[rows, rows],
        out_shape=[jax.ShapeDtypeStruct((N_ROWS, D_MODEL), BF16),
                   jax.ShapeDtypeStruct((N_ROWS, D_MODEL), F32)],
        compiler_params=_params(("parallel",), 40),
        name="entry_norm_modulate",
    )(x_cols, ctx.reshape(N_CTX_ROWS, D_MODEL), gain.reshape(1, D_MODEL), shift, scale)


def _final_norm_kernel(x_ref, g_ref, o_ref):
    x = x_ref[...]
    var = jnp.mean(x * x, axis=-1, keepdims=True)
    y = (x * lax.rsqrt(var + EPS) * g_ref[...]).reshape(ENTRY_COLS, GRID_ROWS, D_MODEL)
    o_ref[...] = pltpu.einshape("crd->rcd", y)


def _final_norm(x, gain):
    tm = ENTRY_COLS * GRID_ROWS
    tiles_per_batch = GRID_W // ENTRY_COLS
    return pl.pallas_call(
        _final_norm_kernel,
        grid=(N_LAT_ROWS // tm,),
        in_specs=[pl.BlockSpec((tm, D_MODEL), lambda i: (i, 0)),
                  pl.BlockSpec((1, D_MODEL), lambda i: (0, 0))],
        out_specs=pl.BlockSpec((None, GRID_ROWS, ENTRY_COLS, D_MODEL),
                               lambda i: (i // tiles_per_batch, 0, i % tiles_per_batch, 0)),
        out_shape=jax.ShapeDtypeStruct((BATCH, GRID_ROWS, GRID_W, D_MODEL), F32),
        compiler_params=_params(("parallel",), 40),
        name="final_norm",
    )(x, gain.reshape(1, D_MODEL))


W2_CAST_ROWS = 64


def _swiglu_up_kernel(z_ref, w1_ref, w3_ref, w2_ref, o_ref, w2_bf16_ref):
    z = z_ref[...]
    h1 = jnp.dot(z, w1_ref[...].astype(BF16), preferred_element_type=F32)
    h3 = jnp.dot(z, w3_ref[...].astype(BF16), preferred_element_type=F32)
    o_ref[...] = (jax.nn.silu(h1) * h3).astype(o_ref.dtype)
    w2_bf16_ref[...] = w2_ref[...].astype(BF16)


def _swiglu_up(z, w1, w3, w2, tm=1024, tn=512):
    m, k = z.shape
    n = w1.shape[1]
    nj = pl.cdiv(n, tn)
    n_w2_blocks = w2.shape[0] // W2_CAST_ROWS
    assert (m // tm) * nj >= n_w2_blocks and w2.shape[0] % W2_CAST_ROWS == 0
    w2_spec = pl.BlockSpec((W2_CAST_ROWS, w2.shape[1]),
                           lambda i, j: (jnp.minimum(i * nj + j, n_w2_blocks - 1), 0))
    return pl.pallas_call(
        _swiglu_up_kernel,
        grid=(m // tm, nj),
        in_specs=[pl.BlockSpec((tm, k), lambda i, j: (i, 0), pipeline_mode=pl.Buffered(1)),
                  pl.BlockSpec((k, tn), lambda i, j: (0, j)),
                  pl.BlockSpec((k, tn), lambda i, j: (0, j)),
                  w2_spec],
        out_specs=[pl.BlockSpec((tm, tn), lambda i, j: (i, j)), w2_spec],
        out_shape=[jax.ShapeDtypeStruct((m, n), BF16), jax.ShapeDtypeStruct(w2.shape, BF16)],
        compiler_params=_params(("arbitrary", "arbitrary"), 56),
        name="swiglu_up",
    )(z, w1, w3, w2)


def _mm_res_kernel(a_ref, w_ref, res_ref, gate_ref, o_ref, *scratch, scale, nk):
    if nk == 1:
        prod = jnp.dot(a_ref[...], w_ref[...].astype(BF16), preferred_element_type=F32)
        o_ref[...] = res_ref[...] + (scale * gate_ref[0]) * prod
        return
    acc_ref, = scratch
    k = pl.program_id(2)

    @pl.when(k == 0)
    def _():
        acc_ref[...] = jnp.zeros_like(acc_ref)

    acc_ref[...] += jnp.dot(a_ref[...], w_ref[...].astype(BF16), preferred_element_type=F32)

    @pl.when(k == nk - 1)
    def _():
        o_ref[...] = res_ref[...] + (scale * gate_ref[0]) * acc_ref[...]


def _mm_res(a, w, res, res_tile_off, gate, scale, group_fn, tm, tn, tk, vmem_mib):
    m, k = a.shape
    n = w.shape[1]
    nk = k // tk
    return pl.pallas_call(
        functools.partial(_mm_res_kernel, scale=scale, nk=nk),
        grid=(m // tm, n // tn, nk),
        in_specs=[pl.BlockSpec((tm, tk), lambda i, j, kk: (i, kk)),
                  pl.BlockSpec((tk, tn), lambda i, j, kk: (kk, j)),
                  pl.BlockSpec((tm, tn), lambda i, j, kk: (i + res_tile_off, j)),
                  pl.BlockSpec((1, 1, tn), lambda i, j, kk: (group_fn(i), 0, j))],
        out_specs=pl.BlockSpec((tm, tn), lambda i, j, kk: (i, j)),
        out_shape=jax.ShapeDtypeStruct((m, n), F32),
        scratch_shapes=[] if nk == 1 else [pltpu.VMEM((tm, tn), F32)],
        compiler_params=_params(("parallel", "parallel", "arbitrary"), vmem_mib),
        name="matmul_gated_residual",
    )(a, w, res, gate)


def _mm_kernel(a_ref, w_ref, o_ref):
    o_ref[...] = jnp.dot(a_ref[...], w_ref[...].astype(BF16), preferred_element_type=F32).astype(o_ref.dtype)


def _mm(a, w, out_dtype, tm, tn, vmem_mib, name):
    m, k = a.shape
    n = w.shape[1]
    return pl.pallas_call(
        _mm_kernel,
        grid=(m // tm, n // tn),
        in_specs=[pl.BlockSpec((tm, k), lambda i, j: (i, 0)),
                  pl.BlockSpec((k, tn), lambda i, j: (0, j))],
        out_specs=pl.BlockSpec((tm, tn), lambda i, j: (i, j)),
        out_shape=jax.ShapeDtypeStruct((m, n), out_dtype),
        compiler_params=_params(("parallel", "arbitrary"), vmem_mib),
        name=name,
    )(a, w)


def _merge_kernel(ya_ref, wa_ref, yb_ref, wb_ref, ga_ref, gb_ref, o_ref):
    pa = jnp.dot(ya_ref[...], wa_ref[...].astype(BF16), preferred_element_type=F32)
    pb = jnp.dot(yb_ref[...], wb_ref[...].astype(BF16), preferred_element_type=F32)
    o_ref[...] = (jax.nn.sigmoid(ga_ref[...]) * pa + jax.nn.sigmoid(gb_ref[...]) * pb).astype(o_ref.dtype)


def _merge(ya, wa, yb, wb, z, ga_col_tile, gb_col_tile, z_row_tile_off, tm=1024, tn=512):
    m = ya.shape[0]
    n = wa.shape[1]
    return pl.pallas_call(
        _merge_kernel,
        grid=(m // tm, n // tn),
        in_specs=[pl.BlockSpec((tm, D_A), lambda i, j: (i, 0)),
                  pl.BlockSpec((D_A, tn), lambda i, j: (0, j)),
                  pl.BlockSpec((tm, D_B), lambda i, j: (i, 0)),
                  pl.BlockSpec((D_B, tn), lambda i, j: (0, j)),
                  pl.BlockSpec((tm, tn), lambda i, j: (i + z_row_tile_off, ga_col_tile + j)),
                  pl.BlockSpec((tm, tn), lambda i, j: (i + z_row_tile_off, gb_col_tile + j))],
        out_specs=pl.BlockSpec((tm, tn), lambda i, j: (i, j)),
        out_shape=jax.ShapeDtypeStruct((m, n), BF16),
        compiler_params=_params(("parallel", "arbitrary"), 48),
        name="gated_merge",
    )(ya, wa, yb, wb, z, z)


S5_PRECISION = lax.Precision.HIGHEST


def _cmul(a, b):
    return a[0] * b[0] - a[1] * b[1], a[0] * b[1] + a[1] * b[0]


def _s5_prep_kernel(ldt_ref, lam_row_ref, lam_col_ref, b1_ref, b2_ref, c1_ref, c2_ref, dt_ref,
                    wst_ref, wc_ref, a16_ref):
    g = pl.program_id(0)
    lh = S5_L * S5_H
    lane_sign = jnp.where(lax.broadcasted_iota(jnp.int32, (1, 2 * S5_P), 1) < S5_P, -1.0, 1.0)
    row_sign = jnp.where(lax.broadcasted_iota(jnp.int32, (2 * S5_P, 1), 0) < S5_P, 1.0, -1.0)
    log2_h = S5_H.bit_length() - 1
    log2_l = S5_L.bit_length() - 1
    t_idx = lax.shift_right_logical(lax.broadcasted_iota(jnp.int32, (1, lh), 1), log2_h)
    b1 = b1_ref[0]
    b2 = b2_ref[0]
    c1 = c1_ref[0]
    c2 = c2_ref[0]
    toep = None
    for d in range(2):
        dt = jnp.exp(jnp.full((1, 1), ldt_ref[d, g], F32))
        lr = lam_row_ref[d, 0]
        li = lam_row_ref[2 + d, 0]
        mag = jnp.exp(lr * dt)
        ab_re, ab_im = mag * jnp.cos(li * dt), mag * jnp.sin(li * dt)
        den = lr * lr + li * li
        nr = ab_re - 1.0
        kr = (nr * lr + ab_im * li) / den
        ki = (ab_im * lr - nr * li) / den
        bk = kr * b1 + lane_sign * ki * b2
        bk_sw = lane_sign * (kr * b2 - lane_sign * ki * b1)
        pw = [(jnp.ones_like(ab_re), jnp.zeros_like(ab_re))]
        for _ in range(S5_L):
            pw.append(_cmul(pw[-1], (ab_re, ab_im)))
        taus = range(S5_L - 1, -1, -1) if d == 0 else range(S5_L)
        rows_of = lambda part: jnp.concatenate(
            [jnp.broadcast_to(pw[tau][part], (S5_H, 2 * S5_P)) for tau in taus], axis=0)
        wst_ref[0, :, d * 2 * S5_P:(d + 1) * 2 * S5_P] = rows_of(0) * bk + rows_of(1) * bk_sw
        a16_ref[0, 2 * d:2 * d + 1, :] = pw[S5_L][0]
        a16_ref[0, 2 * d + 1:2 * d + 2, :] = lane_sign * pw[S5_L][1]
        lrc = lam_col_ref[d, 0]
        lic = lam_col_ref[2 + d, 0]
        magc = jnp.exp(lrc * dt)
        squares = [(magc * jnp.cos(lic * dt), magc * jnp.sin(lic * dt))]
        for _ in range(log2_l - 1):
            squares.append(_cmul(squares[-1], squares[-1]))
        tau_col = t_idx if d == 0 else (S5_L - 1) - t_idx
        q0 = None
        for j, sq in enumerate(squares):
            bit = (lax.shift_right_logical(tau_col, j) & 1) == 1
            factor = (jnp.where(bit, sq[0], 1.0), jnp.where(bit, sq[1], 0.0))
            q0 = factor if q0 is None else _cmul(q0, factor)
        q1 = _cmul(q0, squares[0])

        def c_times(q):
            return row_sign * c1 * q[0] - c2 * q[1]

        wc_ref[0, d * 2 * S5_P:(d + 1) * 2 * S5_P, :] = c_times(q1)
        m_all = jnp.dot(bk[0:S5_H], c_times(q0), precision=S5_PRECISION,
                        preferred_element_type=F32)
        lane = lax.broadcasted_iota(jnp.int32, (S5_H, lh), 1)
        blocks = []
        for s in range(S5_L):
            if d == 0:
                shift, keep = S5_H * s, lane >= S5_H * s
            else:
                shift, keep = (S5_H * (s + 1)) % lh, lane < S5_H * (s + 1)
            rolled = m_all if shift == 0 else pltpu.roll(m_all, shift, axis=1)
            blocks.append(jnp.where(keep, rolled, 0.0))
        td = jnp.concatenate(blocks, axis=0)
        toep = td if toep is None else toep + td
    diag = lax.broadcasted_iota(jnp.int32, (lh, lh), 0) == lax.broadcasted_iota(jnp.int32, (lh, lh), 1)
    wc_ref[0, 4 * S5_P:, :] = toep + jnp.where(diag, dt_ref[0], 0.0)


def _s5_prep(lam_re, lam_im, log_dt, b_re, b_im, c_re, c_im, d_skip):
    lh = S5_L * S5_H
    dup = lambda z: jnp.concatenate([z, z], axis=-1)
    lam = jnp.concatenate([lam_re, lam_im], axis=0)
    lam_row = dup(lam)[:, :, None, :]
    lam_col = dup(lam)[:, :, :, None]
    bt_re, bt_im = jnp.swapaxes(b_re, 1, 2), jnp.swapaxes(b_im, 1, 2)
    b1 = jnp.tile(jnp.concatenate([bt_re, bt_im], axis=-1), (1, S5_L, 1))
    b2 = jnp.tile(jnp.concatenate([bt_im, bt_re], axis=-1), (1, S5_L, 1))
    ct_re, ct_im = jnp.swapaxes(c_re, 1, 2), jnp.swapaxes(c_im, 1, 2)
    c1 = jnp.tile(jnp.concatenate([ct_re, ct_im], axis=1), (1, 1, S5_L))
    c2 = jnp.tile(jnp.concatenate([ct_im, ct_re], axis=1), (1, 1, S5_L))
    d_tiled = jnp.tile(d_skip.reshape(S5_G, 1, S5_H), (1, 1, S5_L))
    grp = lambda shape: pl.BlockSpec((1,) + shape, lambda g: (g, 0, 0))
    return pl.pallas_call(
        _s5_prep_kernel,
        grid=(S5_G,),
        in_specs=[pl.BlockSpec(memory_space=pltpu.SMEM),
                  pl.BlockSpec((4, 1, 1, 2 * S5_P), lambda g: (0, g, 0, 0)),
                  pl.BlockSpec((4, 1, 2 * S5_P, 1), lambda g: (0, g, 0, 0)),
                  grp((lh, 2 * S5_P)), grp((lh, 2 * S5_P)),
                  grp((2 * S5_P, lh)), grp((2 * S5_P, lh)),
                  grp((1, lh))],
        out_specs=[grp((lh, 4 * S5_P)), grp((4 * S5_P + lh, lh)), grp((4, 2 * S5_P))],
        out_shape=[jax.ShapeDtypeStruct((S5_G, lh, 4 * S5_P), F32),
                   jax.ShapeDtypeStruct((S5_G, 4 * S5_P + lh, lh), F32),
                   jax.ShapeDtypeStruct((S5_G, 4, 2 * S5_P), F32)],
        compiler_params=_params(("parallel",), 32),
        name="s5_prep",
    )(log_dt, lam_row, lam_col, b1, b2, c1, c2, d_tiled)


def _s5_local_kernel(ul_ref, uc_ref, wst_ref, xf_ref, xb_ref):
    n_lat = ul_ref.shape[1]
    dot = functools.partial(jnp.dot, precision=S5_PRECISION, preferred_element_type=F32)
    r_lat = dot(ul_ref[0], wst_ref[0])
    r_ctx = dot(uc_ref[0], wst_ref[0])
    xf_ref[0:n_lat, :] = r_lat[:, :2 * S5_P]
    xf_ref[n_lat:, :] = r_ctx[:, :2 * S5_P]
    xb_ref[0:n_lat, :] = r_lat[:, 2 * S5_P:]
    xb_ref[n_lat:, :] = r_ctx[:, 2 * S5_P:]


def _s5_local(ug_lat, ug_ctx, wst):
    n_lat, n_ctx = ug_lat.shape[1], ug_ctx.shape[1]
    lh = S5_L * S5_H
    col = pl.BlockSpec((n_lat + n_ctx, 2 * S5_P), lambda g: (0, g))
    shape = jax.ShapeDtypeStruct((n_lat + n_ctx, S5_G * 2 * S5_P), F32)
    return pl.pallas_call(
        _s5_local_kernel,
        grid=(S5_G,),
        in_specs=[pl.BlockSpec((1, n_lat, lh), lambda g: (g, 0, 0)),
                  pl.BlockSpec((1, n_ctx, lh), lambda g: (g, 0, 0)),
                  pl.BlockSpec((1, lh, 4 * S5_P), lambda g: (g, 0, 0))],
        out_specs=[col, col],
        out_shape=[shape, shape],
        compiler_params=_params(("parallel",), 32),
        name="s5_local_state",
    )(ug_lat, ug_ctx, wst)


S5_SCAN_BLOCK = 8


def _s5_scan_kernel(xf_ref, xb_ref, a_ref, pf_ref, pb_ref, sf_ref, sb_ref):
    @pl.when(pl.program_id(0) == 0)
    def _():
        sf_ref[...] = jnp.zeros_like(sf_ref)
        sb_ref[...] = jnp.zeros_like(sb_ref)

    lanes = sf_ref.shape[1]
    in_re_half = (lax.broadcasted_iota(jnp.int32, (1, lanes), 1) & (2 * S5_P - 1)) < S5_P

    def step(s, x, aa, ab):
        partner = jnp.where(in_re_half, pltpu.roll(s, lanes - S5_P, axis=1), pltpu.roll(s, S5_P, axis=1))
        return aa * s + ab * partner + x

    s = sf_ref[...]
    for j in range(S5_SCAN_BLOCK):
        rows = slice(j * BATCH, (j + 1) * BATCH)
        pf_ref[rows, :] = s
        s = step(s, xf_ref[rows, :], a_ref[0], a_ref[1])
    sf_ref[...] = s
    s = sb_ref[...]
    for j in reversed(range(S5_SCAN_BLOCK)):
        rows = slice(j * BATCH, (j + 1) * BATCH)
        pb_ref[rows, :] = s
        s = step(s, xb_ref[rows, :], a_ref[2], a_ref[3])
    sb_ref[...] = s


def _s5_scan(xf, xb, a16):
    rows, lanes = xf.shape
    blk_rows = S5_SCAN_BLOCK * BATCH
    nblk = rows // blk_rows
    ctx_blk = (CTX_LEN // S5_L) // S5_SCAN_BLOCK
    lat_blk = nblk - ctx_blk
    fwd = lambda i: (jnp.where(i < ctx_blk, lat_blk + i, i - ctx_blk), 0)
    bwd = lambda i: (nblk - 1 - i, 0)
    blk = (blk_rows, lanes)
    shape = jax.ShapeDtypeStruct(xf.shape, F32)
    return pl.pallas_call(
        _s5_scan_kernel,
        grid=(nblk,),
        in_specs=[pl.BlockSpec(blk, fwd), pl.BlockSpec(blk, bwd),
                  pl.BlockSpec((4, 1, lanes), lambda i: (0, 0, 0))],
        out_specs=[pl.BlockSpec(blk, fwd), pl.BlockSpec(blk, bwd)],
        out_shape=[shape, shape],
        scratch_shapes=[pltpu.VMEM((BATCH, lanes), F32), pltpu.VMEM((BATCH, lanes), F32)],
        compiler_params=_params(("arbitrary",), 32),
        name="s5_chunk_scan",
    )(xf, xb, a16)


def _s5_out_kernel(pf_ref, pb_ref, u_ref, wc_ref, y_ref):
    dot = functools.partial(jnp.dot, precision=S5_PRECISION, preferred_element_type=F32)
    y = (dot(pf_ref[...], wc_ref[0, 0:2 * S5_P, :])
         + dot(pb_ref[...], wc_ref[0, 2 * S5_P:4 * S5_P, :])
         + dot(u_ref[0], wc_ref[0, 4 * S5_P:, :]))
    y_ref[0] = jax.nn.gelu(y)


def _s5_out(pf, pb, ug_lat, wc):
    rows = ug_lat.shape[1]
    lh = S5_L * S5_H
    col = pl.BlockSpec((rows, 2 * S5_P), lambda g: (0, g))
    return pl.pallas_call(
        _s5_out_kernel,
        grid=(S5_G,),
        in_specs=[col, col,
                  pl.BlockSpec((1, rows, lh), lambda g: (g, 0, 0)),
                  pl.BlockSpec((1, 4 * S5_P + lh, lh), lambda g: (g, 0, 0))],
        out_specs=pl.BlockSpec((1, rows, lh), lambda g: (g, 0, 0)),
        out_shape=jax.ShapeDtypeStruct((S5_G, rows, lh), F32),
        compiler_params=_params(("parallel",), 32),
        name="s5_output",
    )(pf, pb, ug_lat, wc)


def _s5_glu_kernel(y_ref, w_ref, o_ref):
    y = y_ref[...]
    gate = jnp.dot(y.astype(BF16), w_ref[...].astype(BF16), preferred_element_type=F32)
    o_ref[...] = (y * jax.nn.sigmoid(gate)).astype(o_ref.dtype)


def _s5_glu(y, w_glu, tm=1024):
    m = y.shape[0]
    return pl.pallas_call(
        _s5_glu_kernel,
        grid=(m // tm,),
        in_specs=[pl.BlockSpec((tm, D_A), lambda i: (i, 0)),
                  pl.BlockSpec((D_A, D_A), lambda i: (0, 0))],
        out_specs=pl.BlockSpec((tm, D_A), lambda i: (i, 0)),
        out_shape=jax.ShapeDtypeStruct((m, D_A), BF16),
        compiler_params=_params(("parallel",), 32),
        name="s5_glu",
    )(y, w_glu)


def _s5_branch(z, lam_re, lam_im, log_dt, b_re, b_im, c_re, c_im, d_skip, w_glu):
    wst, wc, a16 = _s5_prep(lam_re, lam_im, log_dt, b_re, b_im, c_re, c_im, d_skip)
    nc_ctx, nc_lat = CTX_LEN // S5_L, SEQ // S5_L
    n_chunks = nc_ctx + nc_lat
    u = z[:, :D_A]
    ug_ctx = u[N_LAT_ROWS:].reshape(BATCH, nc_ctx, S5_L, S5_G, S5_H).transpose(3, 1, 0, 2, 4)
    ug_ctx = ug_ctx.reshape(S5_G, nc_ctx * BATCH, S5_L * S5_H)
    ug_lat = u[:N_LAT_ROWS].reshape(BATCH, GRID_W // S5_L, S5_L, GRID_ROWS, S5_G, S5_H)
    ug_lat = ug_lat.transpose(4, 3, 1, 0, 2, 5).reshape(S5_G, nc_lat * BATCH, S5_L * S5_H)
    xf, xb = _s5_local(ug_lat, ug_ctx, wst)
    a16_lanes = jnp.swapaxes(a16, 0, 1).reshape(4, 1, S5_G * 2 * S5_P)
    pf, pb = _s5_scan(xf, xb, a16_lanes)
    yg = _s5_out(pf, pb, ug_lat, wc)
    y_lat = yg.reshape(S5_G, GRID_ROWS, GRID_W // S5_L, BATCH, S5_L, S5_H)
    y_lat = y_lat.transpose(3, 2, 4, 1, 0, 5).reshape(N_LAT_ROWS, D_A)
    return _s5_glu(y_lat, w_glu)


HG_PREP_ROWS = 256
HG_UNROLL = 12


def _chunk_cumsum(x, reverse):
    n = x.shape[0]
    pos = lax.broadcasted_iota(jnp.int32, (n, 1), 0) & (HG_CHUNK - 1)
    s = 1
    while s < HG_CHUNK:
        if reverse:
            x = x + jnp.where(pos < HG_CHUNK - s, pltpu.roll(x, n - s, axis=0), 0.0)
        else:
            x = x + jnp.where(pos >= s, pltpu.roll(x, s, axis=0), 0.0)
        s *= 2
    return x


def _hgrn_kernel(qc_ref, ql_ref, vc_ref, vl_ref, ffc_ref, ffl_ref, fbc_ref, fbl_ref, og_ref, lg_ref,
                 gn_ref, o_ref, qe_s, ke_s, qb_s, kd_s, v_s, dec_s, of_s, ob_s, *, layer):
    c = HG_CHUNK
    n_chunks = T_ALL // c
    n_ctx_chunks = CTX_LEN // c
    row = lax.broadcasted_iota(jnp.int32, (c, c), 0)
    col = lax.broadcasted_iota(jnp.int32, (c, c), 1)
    causal = row >= col
    anti = row <= col
    nt = (((1,), (1,)), ((), ()))
    tn = (((0,), (0,)), ((), ()))

    def lower_bound(d):
        logits = [lg_ref[d, j, 0] for j in range(lg_ref.shape[1])]
        top = functools.reduce(jnp.maximum, logits)
        e = [jnp.exp(l - top) for l in logits]
        return sum(e[:layer + 1]) / sum(e)

    lb = (lower_bound(0), lower_bound(1))

    def prepare(row0, chunk0, q, v, fpre):
        n = q.shape[0]
        rows = pl.ds(row0, n)
        as_chunks = lambda a: a.reshape(n // c, c, HG_DK)
        flat_bf16 = lambda a: a.reshape(n, HG_DK).astype(BF16)
        v_s[rows, :] = v.astype(BF16)
        q3 = as_chunks(q)
        for d in range(2):
            f = lb[d] + (1.0 - lb[d]) * jax.nn.sigmoid(fpre[d])
            k3 = as_chunks(1.0 - f)
            b3 = as_chunks(_chunk_cumsum(jnp.log(f), reverse=d == 1))
            mid = c // 2 - 1 if d == 0 else c // 2
            last = c - 1 if d == 0 else 0
            b_mid, b_last = b3[:, mid:mid + 1, :], b3[:, last:last + 1, :]
            qe_s[d, rows, :] = flat_bf16(q3 * jnp.exp(b3 - b_mid))
            ke_s[d, rows, :] = flat_bf16(k3 * jnp.exp(b_mid - b3))
            qb_s[d, rows, :] = flat_bf16(q3 * jnp.exp(b3))
            kd_s[d, rows, :] = flat_bf16(k3 * jnp.exp(b_last - b3))
            dec_s[d, pl.ds(chunk0, n // c)] = jnp.exp(b_last)

    prepare(0, 0, qc_ref[...], vc_ref[...], (ffc_ref[...], fbc_ref[...]))

    def prep_body(blk, carry):
        r0 = pl.multiple_of(blk * HG_PREP_ROWS, HG_PREP_ROWS)
        src = pl.ds(r0, HG_PREP_ROWS)
        prepare(pl.multiple_of(CTX_LEN + r0, HG_PREP_ROWS), n_ctx_chunks + blk * (HG_PREP_ROWS // c),
                ql_ref[src, :], vl_ref[src, :], (ffl_ref[src, :], fbl_ref[src, :]))
        return carry

    lax.fori_loop(0, SEQ // HG_PREP_ROWS, prep_body, 0)

    def chunk(cidx, d, state_t):
        rows = pl.ds(pl.multiple_of(cidx * c, c), c)
        v = v_s[rows, :]
        scores = lax.dot_general(qe_s[d, rows, :], ke_s[d, rows, :], nt, preferred_element_type=F32)
        scores = jnp.where(causal if d == 0 else anti, scores, 0.0)
        o = jnp.dot(scores.astype(BF16), v, preferred_element_type=F32)
        o = o + lax.dot_general(qb_s[d, rows, :], state_t.astype(BF16), nt, preferred_element_type=F32)
        kv_t = lax.dot_general(v, kd_s[d, rows, :], tn, preferred_element_type=F32)
        (of_s if d == 0 else ob_s)[rows, :] = o
        return state_t * dec_s[d, cidx] + kv_t

    def body(i, carry):
        sf, sb = carry
        sf = chunk(i, 0, sf)
        jb = jnp.where(i < n_ctx_chunks, n_ctx_chunks - 1 - i, n_chunks - 1 + n_ctx_chunks - i)
        sb = chunk(jb, 1, sb)
        return sf, sb

    zero = jnp.zeros((HG_DK, HG_DK), F32)
    lax.fori_loop(0, n_chunks, body, (zero, zero), unroll=HG_UNROLL)

    o = of_s[CTX_LEN:, :] + ob_s[CTX_LEN:, :]
    o = o * lax.rsqrt(jnp.mean(o * o, axis=-1, keepdims=True) + EPS)
    o_ref[...] = (o * gn_ref[0] * jax.nn.silu(og_ref[...])).astype(o_ref.dtype)


def _hgrn_branch(z, lb_logits, hg_norm, layer):
    n_slots = lb_logits.shape[1]
    ctx_blk0 = N_LAT_ROWS // CTX_LEN

    def windows(i):
        col_blk = (D_A + i * D_B) // HG_DK
        return [pl.BlockSpec((CTX_LEN, HG_DK), lambda b, h: (ctx_blk0 + b, col_blk + h)),
                pl.BlockSpec((SEQ, HG_DK), lambda b, h: (b, col_blk + h))]

    in_specs = [spec for i in range(4) for spec in windows(i)]
    in_specs += [windows(4)[1],
                 pl.BlockSpec((2, n_slots, 1, 1, HG_DK), lambda b, h: (0, 0, h, 0, 0)),
                 pl.BlockSpec((1, 1, HG_DK), lambda b, h: (h, 0, 0))]
    operand = lambda: pltpu.VMEM((2, T_ALL, HG_DK), BF16)
    lg = lb_logits.astype(F32).reshape(2, n_slots, HG_HEADS, 1, HG_DK)
    return pl.pallas_call(
        functools.partial(_hgrn_kernel, layer=layer),
        grid=(BATCH, HG_HEADS),
        in_specs=in_specs,
        out_specs=pl.BlockSpec((SEQ, HG_DK), lambda b, h: (b, h)),
        out_shape=jax.ShapeDtypeStruct((N_LAT_ROWS, D_B), BF16),
        scratch_shapes=[operand(), operand(), operand(), operand(),
                        pltpu.VMEM((T_ALL, HG_DK), BF16),
                        pltpu.VMEM((2, T_ALL // HG_CHUNK, 1, HG_DK), F32),
                        pltpu.VMEM((T_ALL, HG_DK), F32),
                        pltpu.VMEM((T_ALL, HG_DK), F32)],
        compiler_params=_params(("parallel", "parallel"), 40),
        name="hgrn2",
    )(*([z] * 9), lg, hg_norm.reshape(HG_HEADS, 1, HG_DK))


FF_DOWN_TILE = 512


def kernel(x, c, ctx, c_ctx, w_ada, b_ada, norm_ffn1, w1_ffn1, w3_ffn1, w2_ffn1, norm_mix, w_in,
           s5_lam_re, s5_lam_im, s5_log_dt, s5_b_re, s5_b_im, s5_c_re, s5_c_im, s5_d, s5_w_glu,
           hg_lb_logits, hg_norm, w_proj_a, w_proj_b, w_out, norm_ffn2, w1_ffn2, w3_ffn2, w2_ffn2,
           norm_final):
    layer = 0
    c8 = jnp.concatenate([c, c_ctx[None], jnp.zeros((8 - BATCH - 1, D_MODEL), F32)], axis=0)
    mod = _ada(c8, w_ada[layer], b_ada[layer]).reshape(8, N_MOD, D_MODEL)
    mod = jnp.concatenate([mod[BATCH:BATCH + 1], mod[:BATCH]], axis=0)
    m = [mod[:, i].reshape(BATCH + 1, 1, D_MODEL) for i in range(N_MOD)]

    tile = 1024

    z1, h0 = _entry(x, ctx, norm_ffn1[layer], m[0], m[1])
    g1, w2_bf16 = _swiglu_up(z1, w1_ffn1[layer], w3_ffn1[layer], w2_ffn1[layer])
    h1 = _mm_res(g1, w2_bf16, h0, 0, m[2], 0.5, _group_full(FF_DOWN_TILE),
                 FF_DOWN_TILE, FF_DOWN_TILE, D_FF, 56)

    z2 = _norm_mod(h1, norm_mix[layer], m[3], m[4], _group_full(512))
    z = _mm(z2, w_in[layer], F32, tile, 512, 52, "mixer_in_proj")

    y_a = _s5_branch(z, s5_lam_re[layer], s5_lam_im[layer], s5_log_dt[layer], s5_b_re[layer],
                     s5_b_im[layer], s5_c_re[layer], s5_c_im[layer], s5_d[layer], s5_w_glu[layer])
    y_b = _hgrn_branch(z, hg_lb_logits, hg_norm[layer], layer)

    ga_tile = (D_A + 5 * D_B) // 512
    merged = _merge(y_a, w_proj_a[layer], y_b, w_proj_b[layer], z,
                    ga_tile, ga_tile + D_MODEL // 512, 0)
    h2 = _mm_res(merged, w_out[layer], h1, 0, m[5], 1.0, _group_lat(tile),
                 tile, 512, D_MODEL, 52)

    z3 = _norm_mod(h2, norm_ffn2[layer], m[6], m[7], _group_lat(512))
    g2, w2_bf16 = _swiglu_up(z3, w1_ffn2[layer], w3_ffn2[layer], w2_ffn2[layer])
    h3 = _mm_res(g2, w2_bf16, h2, 0, m[8], 0.5, _group_lat(FF_DOWN_TILE),
                 FF_DOWN_TILE, FF_DOWN_TILE, D_FF, 56)
    return _final_norm(h3, norm_final).reshape(BATCH, SEQ, D_MODEL)
```

```python
import functools
import math

import jax
import jax.numpy as jnp
from jax import lax
from jax.experimental import pallas as pl
from jax.experimental.pallas import tpu as pltpu

F32 = jnp.float32
BF16 = jnp.bfloat16

D_MODEL = 4096
BATCH = 4
SEQ = 2048
GRID_W = 64
GRID_ROWS = SEQ // GRID_W
CTX_LEN = 256
T_ALL = CTX_LEN + SEQ
D_FF = 11008
D_A = 1024
S5_H = 16
S5_G = D_A // S5_H
S5_P = 64
S5_L = 16
D_B = 2048
HG_DK = 128
HG_HEADS = D_B // HG_DK
HG_CHUNK = 64
N_MOD = 9
N_IN = D_A + 5 * D_B + 2 * D_MODEL
EPS = 1e-6

N_CTX_ROWS = BATCH * CTX_LEN
N_LAT_ROWS = BATCH * SEQ
N_ROWS = N_CTX_ROWS + N_LAT_ROWS
MIB = 1024 * 1024


def _params(semantics, vmem_mib):
    return pltpu.CompilerParams(dimension_semantics=semantics, vmem_limit_bytes=vmem_mib * MIB)


def _group_full(tm):
    n_lat_tiles, per_batch = N_LAT_ROWS // tm, SEQ // tm
    return lambda i: jnp.where(i < n_lat_tiles, 1 + i // per_batch, 0)


def _group_lat(tm):
    per_batch = SEQ // tm
    return lambda i: 1 + i // per_batch


def _ada_kernel(c_ref, w_ref, b_ref, o_ref):
    sc = jax.nn.silu(c_ref[...]).astype(BF16)
    o_ref[...] = jnp.dot(sc, w_ref[...].astype(BF16), preferred_element_type=F32) + b_ref[...]


def _ada(c8, w_ada, b_ada):
    n = w_ada.shape[1]
    tn = 1024
    return pl.pallas_call(
        _ada_kernel,
        grid=(n // tn,),
        in_specs=[pl.BlockSpec((8, D_MODEL), lambda j: (0, 0)),
                  pl.BlockSpec((D_MODEL, tn), lambda j: (0, j)),
                  pl.BlockSpec((1, tn), lambda j: (0, j))],
        out_specs=pl.BlockSpec((8, tn), lambda j: (0, j)),
        out_shape=jax.ShapeDtypeStruct((8, n), F32),
        compiler_params=_params(("parallel",), 48),
        name="ada_table",
    )(c8, w_ada, b_ada.reshape(1, n))


def _norm_mod_kernel(x_ref, g_ref, sh_ref, sc_ref, o_ref):
    x = x_ref[...]
    var = jnp.mean(x * x, axis=-1, keepdims=True)
    y = x * lax.rsqrt(var + EPS) * g_ref[...]
    o_ref[...] = (y * (1.0 + sc_ref[0]) + sh_ref[0]).astype(o_ref.dtype)


def _norm_mod(x, gain, shift, scale, group_fn, tm=512):
    rows = x.shape[0]
    vec = pl.BlockSpec((1, 1, D_MODEL), lambda i: (group_fn(i), 0, 0))
    return pl.pallas_call(
        _norm_mod_kernel,
        grid=(rows // tm,),
        in_specs=[pl.BlockSpec((tm, D_MODEL), lambda i: (i, 0)),
                  pl.BlockSpec((1, D_MODEL), lambda i: (0, 0)),
                  vec, vec],
        out_specs=pl.BlockSpec((tm, D_MODEL), lambda i: (i, 0)),
        out_shape=jax.ShapeDtypeStruct((rows, D_MODEL), BF16),
        compiler_params=_params(("parallel",), 40),
        name="norm_modulate",
    )(x, gain.reshape(1, D_MODEL), shift, scale)


ENTRY_COLS = 8


def _entry_kernel(x_ref, c_ref, g_ref, sh_ref, sc_ref, z_ref, h_ref, *, n_lat_tiles):
    def emit(rows, x):
        var = jnp.mean(x * x, axis=-1, keepdims=True)
        y = x * lax.rsqrt(var + EPS) * g_ref[...]
        z_ref[rows, :] = (y * (1.0 + sc_ref[0]) + sh_ref[0]).astype(z_ref.dtype)
        h_ref[rows, :] = x

    @pl.when(pl.program_id(0) < n_lat_tiles)
    def _():
        x = pltpu.einshape("rcd->crd", x_ref[...])
        emit(slice(None), x.reshape(ENTRY_COLS * GRID_ROWS, D_MODEL))

    @pl.when(pl.program_id(0) >= n_lat_tiles)
    def _():
        emit(slice(None), c_ref[...])


def _entry(x, ctx, gain, shift, scale):
    tm = ENTRY_COLS * GRID_ROWS
    tiles_per_batch = GRID_W // ENTRY_COLS
    n_lat_tiles = BATCH * tiles_per_batch
    lat = lambda i: jnp.minimum(i, n_lat_tiles - 1)
    group = lambda i: jnp.where(i < n_lat_tiles, 1 + i // tiles_per_batch, 0)
    vec = pl.BlockSpec((1, 1, D_MODEL), lambda i: (group(i), 0, 0))
    rows = pl.BlockSpec((tm, D_MODEL), lambda i: (i, 0))
    x_cols = x.reshape(BATCH, GRID_ROWS, GRID_W, D_MODEL)
    return pl.pallas_call(
        functools.partial(_entry_kernel, n_lat_tiles=n_lat_tiles),
        grid=(N_ROWS // tm,),
        in_specs=[pl.BlockSpec((None, GRID_ROWS, ENTRY_COLS, D_MODEL),
                               lambda i: (lat(i) // tiles_per_batch, 0, lat(i) % tiles_per_batch, 0)),
                  pl.BlockSpec((tm, D_MODEL), lambda i: (jnp.maximum(i - n_lat_tiles, 0), 0)),
                  pl.BlockSpec((1, D_MODEL), lambda i: (0, 0)),
                  vec, vec],
        out_specs=[rows, rows],
        out_shape=[jax.ShapeDtypeStruct((N_ROWS, D_MODEL), BF16),
                   jax.ShapeDtypeStruct((N_ROWS, D_MODEL), F32)],
        compiler_params=_params(("parallel",), 40),
        name="entry_norm_modulate",
    )(x_cols, ctx.reshape(N_CTX_ROWS, D_MODEL), gain.reshape(1, D_MODEL), shift, scale)


def _final_norm_kernel(x_ref, g_ref, o_ref):
    x = x_ref[...]
    var = jnp.mean(x * x, axis=-1, keepdims=True)
    y = (x * lax.rsqrt(var + EPS) * g_ref[...]).reshape(ENTRY_COLS, GRID_ROWS, D_MODEL)
    o_ref[...] = pltpu.einshape("crd->rcd", y)


def _final_norm(x, gain):
    tm = ENTRY_COLS * GRID_ROWS
    tiles_per_batch = GRID_W // ENTRY_COLS
    return pl.pallas_call(
        _final_norm_kernel,
        grid=(N_LAT_ROWS // tm,),
        in_specs=[pl.BlockSpec((tm, D_MODEL), lambda i: (i, 0)),
                  pl.BlockSpec((1, D_MODEL), lambda i: (0, 0))],
        out_specs=pl.BlockSpec((None, GRID_ROWS, ENTRY_COLS, D_MODEL),
                               lambda i: (i // tiles_per_batch, 0, i % tiles_per_batch, 0)),
        out_shape=jax.ShapeDtypeStruct((BATCH, GRID_ROWS, GRID_W, D_MODEL), F32),
        compiler_params=_params(("parallel",), 40),
        name="final_norm",
    )(x, gain.reshape(1, D_MODEL))


W2_CAST_ROWS = 64


def _swiglu_up_kernel(z_ref, w1_ref, w3_ref, w2_ref, o_ref, w2_bf16_ref):
    z = z_ref[...]
    h1 = jnp.dot(z, w1_ref[...].astype(BF16), preferred_element_type=F32)
    h3 = jnp.dot(z, w3_ref[...].astype(BF16), preferred_element_type=F32)
    o_ref[...] = (jax.nn.silu(h1) * h3).astype(o_ref.dtype)
    w2_bf16_ref[...] = w2_ref[...].astype(BF16)


def _swiglu_up(z, w1, w3, w2, tm=1024, tn=512):
    m, k = z.shape
    n = w1.shape[1]
    nj = pl.cdiv(n, tn)
    n_w2_blocks = w2.shape[0] // W2_CAST_ROWS
    assert (m // tm) * nj >= n_w2_blocks and w2.shape[0] % W2_CAST_ROWS == 0
    w2_spec = pl.BlockSpec((W2_CAST_ROWS, w2.shape[1]),
                           lambda i, j: (jnp.minimum(i * nj + j, n_w2_blocks - 1), 0))
    return pl.pallas_call(
        _swiglu_up_kernel,
        grid=(m // tm, nj),
        in_specs=[pl.BlockSpec((tm, k), lambda i, j: (i, 0), pipeline_mode=pl.Buffered(1)),
                  pl.BlockSpec((k, tn), lambda i, j: (0, j)),
                  pl.BlockSpec((k, tn), lambda i, j: (0, j)),
                  w2_spec],
        out_specs=[pl.BlockSpec((tm, tn), lambda i, j: (i, j)), w2_spec],
        out_shape=[jax.ShapeDtypeStruct((m, n), BF16), jax.ShapeDtypeStruct(w2.shape, BF16)],
        compiler_params=_params(("arbitrary", "arbitrary"), 56),
        name="swiglu_up",
    )(z, w1, w3, w2)


def _mm_res_kernel(a_ref, w_ref, res_ref, gate_ref, o_ref, *scratch, scale, nk):
    if nk == 1:
        prod = jnp.dot(a_ref[...], w_ref[...].astype(BF16), preferred_element_type=F32)
        o_ref[...] = res_ref[...] + (scale * gate_ref[0]) * prod
        return
    acc_ref, = scratch
    k = pl.program_id(2)

    @pl.when(k == 0)
    def _():
        acc_ref[...] = jnp.zeros_like(acc_ref)

    acc_ref[...] += jnp.dot(a_ref[...], w_ref[...].astype(BF16), preferred_element_type=F32)

    @pl.when(k == nk - 1)
    def _():
        o_ref[...] = res_ref[...] + (scale * gate_ref[0]) * acc_ref[...]


def _mm_res(a, w, res, res_tile_off, gate, scale, group_fn, tm, tn, tk, vmem_mib):
    m, k = a.shape
    n = w.shape[1]
    nk = k // tk
    return pl.pallas_call(
        functools.partial(_mm_res_kernel, scale=scale, nk=nk),
        grid=(m // tm, n // tn, nk),
        in_specs=[pl.BlockSpec((tm, tk), lambda i, j, kk: (i, kk)),
                  pl.BlockSpec((tk, tn), lambda i, j, kk: (kk, j)),
                  pl.BlockSpec((tm, tn), lambda i, j, kk: (i + res_tile_off, j)),
                  pl.BlockSpec((1, 1, tn), lambda i, j, kk: (group_fn(i), 0, j))],
        out_specs=pl.BlockSpec((tm, tn), lambda i, j, kk: (i, j)),
        out_shape=jax.ShapeDtypeStruct((m, n), F32),
        scratch_shapes=[] if nk == 1 else [pltpu.VMEM((tm, tn), F32)],
        compiler_params=_params(("parallel", "parallel", "arbitrary"), vmem_mib),
        name="matmul_gated_residual",
    )(a, w, res, gate)


def _mm_kernel(a_ref, w_ref, o_ref):
    o_ref[...] = jnp.dot(a_ref[...], w_ref[...].astype(BF16), preferred_element_type=F32).astype(o_ref.dtype)


def _mm(a, w, out_dtype, tm, tn, vmem_mib, name):
    m, k = a.shape
    n = w.shape[1]
    return pl.pallas_call(
        _mm_kernel,
        grid=(m // tm, n // tn),
        in_specs=[pl.BlockSpec((tm, k), lambda i, j: (i, 0)),
                  pl.BlockSpec((k, tn), lambda i, j: (0, j))],
        out_specs=pl.BlockSpec((tm, tn), lambda i, j: (i, j)),
        out_shape=jax.ShapeDtypeStruct((m, n), out_dtype),
        compiler_params=_params(("parallel", "arbitrary"), vmem_mib),
        name=name,
    )(a, w)


def _merge_kernel(ya_ref, wa_ref, yb_ref, wb_ref, ga_ref, gb_ref, o_ref):
    pa = jnp.dot(ya_ref[...], wa_ref[...].astype(BF16), preferred_element_type=F32)
    pb = jnp.dot(yb_ref[...], wb_ref[...].astype(BF16), preferred_element_type=F32)
    o_ref[...] = (jax.nn.sigmoid(ga_ref[...]) * pa + jax.nn.sigmoid(gb_ref[...]) * pb).astype(o_ref.dtype)


def _merge(ya, wa, yb, wb, z, ga_col_tile, gb_col_tile, z_row_tile_off, tm=1024, tn=512):
    m = ya.shape[0]
    n = wa.shape[1]
    return pl.pallas_call(
        _merge_kernel,
        grid=(m // tm, n // tn),
        in_specs=[pl.BlockSpec((tm, D_A), lambda i, j: (i, 0)),
                  pl.BlockSpec((D_A, tn), lambda i, j: (0, j)),
                  pl.BlockSpec((tm, D_B), lambda i, j: (i, 0)),
                  pl.BlockSpec((D_B, tn), lambda i, j: (0, j)),
                  pl.BlockSpec((tm, tn), lambda i, j: (i + z_row_tile_off, ga_col_tile + j)),
                  pl.BlockSpec((tm, tn), lambda i, j: (i + z_row_tile_off, gb_col_tile + j))],
        out_specs=pl.BlockSpec((tm, tn), lambda i, j: (i, j)),
        out_shape=jax.ShapeDtypeStruct((m, n), BF16),
        compiler_params=_params(("parallel", "arbitrary"), 48),
        name="gated_merge",
    )(ya, wa, yb, wb, z, z)


S5_PRECISION = lax.Precision.HIGHEST


def _cmul(a, b):
    return a[0] * b[0] - a[1] * b[1], a[0] * b[1] + a[1] * b[0]


def _s5_prep_kernel(ldt_ref, lam_row_ref, lam_col_ref, b1_ref, b2_ref, c1_ref, c2_ref, dt_ref,
                    wst_ref, wc_ref, a16_ref):
    g = pl.program_id(0)
    lh = S5_L * S5_H
    lane_sign = jnp.where(lax.broadcasted_iota(jnp.int32, (1, 2 * S5_P), 1) < S5_P, -1.0, 1.0)
    row_sign = jnp.where(lax.broadcasted_iota(jnp.int32, (2 * S5_P, 1), 0) < S5_P, 1.0, -1.0)
    log2_h = S5_H.bit_length() - 1
    log2_l = S5_L.bit_length() - 1
    t_idx = lax.shift_right_logical(lax.broadcasted_iota(jnp.int32, (1, lh), 1), log2_h)
    b1 = b1_ref[0]
    b2 = b2_ref[0]
    c1 = c1_ref[0]
    c2 = c2_ref[0]
    toep = None
    for d in range(2):
        dt = jnp.exp(jnp.full((1, 1), ldt_ref[d, g], F32))
        lr = lam_row_ref[d, 0]
        li = lam_row_ref[2 + d, 0]
        mag = jnp.exp(lr * dt)
        ab_re, ab_im = mag * jnp.cos(li * dt), mag * jnp.sin(li * dt)
        den = lr * lr + li * li
        nr = ab_re - 1.0
        kr = (nr * lr + ab_im * li) / den
        ki = (ab_im * lr - nr * li) / den
        bk = kr * b1 + lane_sign * ki * b2
        bk_sw = lane_sign * (kr * b2 - lane_sign * ki * b1)
        pw = [(jnp.ones_like(ab_re), jnp.zeros_like(ab_re))]
        for _ in range(S5_L):
            pw.append(_cmul(pw[-1], (ab_re, ab_im)))
        taus = range(S5_L - 1, -1, -1) if d == 0 else range(S5_L)
        rows_of = lambda part: jnp.concatenate(
            [jnp.broadcast_to(pw[tau][part], (S5_H, 2 * S5_P)) for tau in taus], axis=0)
        wst_ref[0, :, d * 2 * S5_P:(d + 1) * 2 * S5_P] = rows_of(0) * bk + rows_of(1) * bk_sw
        a16_ref[0, 2 * d:2 * d + 1, :] = pw[S5_L][0]
        a16_ref[0, 2 * d + 1:2 * d + 2, :] = lane_sign * pw[S5_L][1]
        lrc = lam_col_ref[d, 0]
        lic = lam_col_ref[2 + d, 0]
        magc = jnp.exp(lrc * dt)
        squares = [(magc * jnp.cos(lic * dt), magc * jnp.sin(lic * dt))]
        for _ in range(log2_l - 1):
            squares.append(_cmul(squares[-1], squares[-1]))
        tau_col = t_idx if d == 0 else (S5_L - 1) - t_idx
        q0 = None
        for j, sq in enumerate(squares):
            bit = (lax.shift_right_logical(tau_col, j) & 1) == 1
            factor = (jnp.where(bit, sq[0], 1.0), jnp.where(bit, sq[1], 0.0))
            q0 = factor if q0 is None else _cmul(q0, factor)
        q1 = _cmul(q0, squares[0])

        def c_times(q):
            return row_sign * c1 * q[0] - c2 * q[1]

        wc_ref[0, d * 2 * S5_P:(d + 1) * 2 * S5_P, :] = c_times(q1)
        m_all = jnp.dot(bk[0:S5_H], c_times(q0), precision=S5_PRECISION,
                        preferred_element_type=F32)
        lane = lax.broadcasted_iota(jnp.int32, (S5_H, lh), 1)
        blocks = []
        for s in range(S5_L):
            if d == 0:
                shift, keep = S5_H * s, lane >= S5_H * s
            else:
                shift, keep = (S5_H * (s + 1)) % lh, lane < S5_H * (s + 1)
            rolled = m_all if shift == 0 else pltpu.roll(m_all, shift, axis=1)
            blocks.append(jnp.where(keep, rolled, 0.0))
        td = jnp.concatenate(blocks, axis=0)
        toep = td if toep is None else toep + td
    diag = lax.broadcasted_iota(jnp.int32, (lh, lh), 0) == lax.broadcasted_iota(jnp.int32, (lh, lh), 1)
    wc_ref[0, 4 * S5_P:, :] = toep + jnp.where(diag, dt_ref[0], 0.0)


def _s5_prep(lam_re, lam_im, log_dt, b_re, b_im, c_re, c_im, d_skip):
    lh = S5_L * S5_H
    dup = lambda z: jnp.concatenate([z, z], axis=-1)
    lam = jnp.concatenate([lam_re, lam_im], axis=0)
    lam_row = dup(lam)[:, :, None, :]
    lam_col = dup(lam)[:, :, :, None]
    bt_re, bt_im = jnp.swapaxes(b_re, 1, 2), jnp.swapaxes(b_im, 1, 2)
    b1 = jnp.tile(jnp.concatenate([bt_re, bt_im], axis=-1), (1, S5_L, 1))
    b2 = jnp.tile(jnp.concatenate([bt_im, bt_re], axis=-1), (1, S5_L, 1))
    ct_re, ct_im = jnp.swapaxes(c_re, 1, 2), jnp.swapaxes(c_im, 1, 2)
    c1 = jnp.tile(jnp.concatenate([ct_re, ct_im], axis=1), (1, 1, S5_L))
    c2 = jnp.tile(jnp.concatenate([ct_im, ct_re], axis=1), (1, 1, S5_L))
    d_tiled = jnp.tile(d_skip.reshape(S5_G, 1, S5_H), (1, 1, S5_L))
    grp = lambda shape: pl.BlockSpec((1,) + shape, lambda g: (g, 0, 0))
    return pl.pallas_call(
        _s5_prep_kernel,
        grid=(S5_G,),
        in_specs=[pl.BlockSpec(memory_space=pltpu.SMEM),
                  pl.BlockSpec((4, 1, 1, 2 * S5_P), lambda g: (0, g, 0, 0)),
                  pl.BlockSpec((4, 1, 2 * S5_P, 1), lambda g: (0, g, 0, 0)),
                  grp((lh, 2 * S5_P)), grp((lh, 2 * S5_P)),
                  grp((2 * S5_P, lh)), grp((2 * S5_P, lh)),
                  grp((1, lh))],
        out_specs=[grp((lh, 4 * S5_P)), grp((4 * S5_P + lh, lh)), grp((4, 2 * S5_P))],
        out_shape=[jax.ShapeDtypeStruct((S5_G, lh, 4 * S5_P), F32),
                   jax.ShapeDtypeStruct((S5_G, 4 * S5_P + lh, lh), F32),
                   jax.ShapeDtypeStruct((S5_G, 4, 2 * S5_P), F32)],
        compiler_params=_params(("parallel",), 32),
        name="s5_prep",
    )(log_dt, lam_row, lam_col, b1, b2, c1, c2, d_tiled)


def _dot_bf16x3(a, b):
    a_hi, b_hi = a.astype(BF16), b.astype(BF16)
    a_lo = (a - a_hi.astype(F32)).astype(BF16)
    b_lo = (b - b_hi.astype(F32)).astype(BF16)
    dot = functools.partial(jnp.dot, preferred_element_type=F32)
    return dot(a_hi, b_hi) + (dot(a_hi, b_lo) + dot(a_lo, b_hi))


def _s5_local_kernel(ul_ref, uc_ref, wst_ref, xf_ref, xb_ref):
    n_lat = ul_ref.shape[1]
    r_lat = _dot_bf16x3(ul_ref[0], wst_ref[0])
    r_ctx = _dot_bf16x3(uc_ref[0], wst_ref[0])
    xf_ref[0:n_lat, :] = r_lat[:, :2 * S5_P]
    xf_ref[n_lat:, :] = r_ctx[:, :2 * S5_P]
    xb_ref[0:n_lat, :] = r_lat[:, 2 * S5_P:]
    xb_ref[n_lat:, :] = r_ctx[:, 2 * S5_P:]


def _s5_local(ug_lat, ug_ctx, wst):
    n_lat, n_ctx = ug_lat.shape[1], ug_ctx.shape[1]
    lh = S5_L * S5_H
    col = pl.BlockSpec((n_lat + n_ctx, 2 * S5_P), lambda g: (0, g))
    shape = jax.ShapeDtypeStruct((n_lat + n_ctx, S5_G * 2 * S5_P), F32)
    return pl.pallas_call(
        _s5_local_kernel,
        grid=(S5_G,),
        in_specs=[pl.BlockSpec((1, n_lat, lh), lambda g: (g, 0, 0)),
                  pl.BlockSpec((1, n_ctx, lh), lambda g: (g, 0, 0)),
                  pl.BlockSpec((1, lh, 4 * S5_P), lambda g: (g, 0, 0))],
        out_specs=[col, col],
        out_shape=[shape, shape],
        compiler_params=_params(("parallel",), 32),
        name="s5_local_state",
    )(ug_lat, ug_ctx, wst)


S5_SCAN_BLOCK = 8


def _s5_scan_kernel(xf_ref, xb_ref, a_ref, pf_ref, pb_ref, sf_ref, sb_ref):
    @pl.when(pl.program_id(0) == 0)
    def _():
        sf_ref[...] = jnp.zeros_like(sf_ref)
        sb_ref[...] = jnp.zeros_like(sb_ref)

    lanes = sf_ref.shape[1]
    in_re_half = (lax.broadcasted_iota(jnp.int32, (1, lanes), 1) & (2 * S5_P - 1)) < S5_P

    def step(s, x, aa, ab):
        partner = jnp.where(in_re_half, pltpu.roll(s, lanes - S5_P, axis=1), pltpu.roll(s, S5_P, axis=1))
        return aa * s + ab * partner + x

    s = sf_ref[...]
    for j in range(S5_SCAN_BLOCK):
        rows = slice(j * BATCH, (j + 1) * BATCH)
        pf_ref[rows, :] = s
        s = step(s, xf_ref[rows, :], a_ref[0], a_ref[1])
    sf_ref[...] = s
    s = sb_ref[...]
    for j in reversed(range(S5_SCAN_BLOCK)):
        rows = slice(j * BATCH, (j + 1) * BATCH)
        pb_ref[rows, :] = s
        s = step(s, xb_ref[rows, :], a_ref[2], a_ref[3])
    sb_ref[...] = s


def _s5_scan(xf, xb, a16):
    rows, lanes = xf.shape
    blk_rows = S5_SCAN_BLOCK * BATCH
    nblk = rows // blk_rows
    ctx_blk = (CTX_LEN // S5_L) // S5_SCAN_BLOCK
    lat_blk = nblk - ctx_blk
    fwd = lambda i: (jnp.where(i < ctx_blk, lat_blk + i, i - ctx_blk), 0)
    bwd = lambda i: (nblk - 1 - i, 0)
    blk = (blk_rows, lanes)
    shape = jax.ShapeDtypeStruct(xf.shape, F32)
    return pl.pallas_call(
        _s5_scan_kernel,
        grid=(nblk,),
        in_specs=[pl.BlockSpec(blk, fwd), pl.BlockSpec(blk, bwd),
                  pl.BlockSpec((4, 1, lanes), lambda i: (0, 0, 0))],
        out_specs=[pl.BlockSpec(blk, fwd), pl.BlockSpec(blk, bwd)],
        out_shape=[shape, shape],
        scratch_shapes=[pltpu.VMEM((BATCH, lanes), F32), pltpu.VMEM((BATCH, lanes), F32)],
        compiler_params=_params(("arbitrary",), 32),
        name="s5_chunk_scan",
    )(xf, xb, a16)


def _s5_out_kernel(pf_ref, pb_ref, u_ref, wc_ref, y_ref):
    lhs = jnp.concatenate([pf_ref[...], pb_ref[...], u_ref[0]], axis=1)
    y_ref[0] = jax.nn.gelu(_dot_bf16x3(lhs, wc_ref[0]))


def _s5_out(pf, pb, ug_lat, wc):
    rows = ug_lat.shape[1]
    lh = S5_L * S5_H
    col = pl.BlockSpec((rows, 2 * S5_P), lambda g: (0, g))
    return pl.pallas_call(
        _s5_out_kernel,
        grid=(S5_G,),
        in_specs=[col, col,
                  pl.BlockSpec((1, rows, lh), lambda g: (g, 0, 0)),
                  pl.BlockSpec((1, 4 * S5_P + lh, lh), lambda g: (g, 0, 0))],
        out_specs=pl.BlockSpec((1, rows, lh), lambda g: (g, 0, 0)),
        out_shape=jax.ShapeDtypeStruct((S5_G, rows, lh), F32),
        compiler_params=_params(("parallel",), 32),
        name="s5_output",
    )(pf, pb, ug_lat, wc)


def _s5_glu_kernel(y_ref, w_ref, o_ref):
    y = y_ref[...]
    gate = jnp.dot(y.astype(BF16), w_ref[...].astype(BF16), preferred_element_type=F32)
    o_ref[...] = (y * jax.nn.sigmoid(gate)).astype(o_ref.dtype)


def _s5_glu(y, w_glu, tm=1024):
    m = y.shape[0]
    return pl.pallas_call(
        _s5_glu_kernel,
        grid=(m // tm,),
        in_specs=[pl.BlockSpec((tm, D_A), lambda i: (i, 0)),
                  pl.BlockSpec((D_A, D_A), lambda i: (0, 0))],
        out_specs=pl.BlockSpec((tm, D_A), lambda i: (i, 0)),
        out_shape=jax.ShapeDtypeStruct((m, D_A), BF16),
        compiler_params=_params(("parallel",), 32),
        name="s5_glu",
    )(y, w_glu)


def _s5_branch(z, lam_re, lam_im, log_dt, b_re, b_im, c_re, c_im, d_skip, w_glu):
    wst, wc, a16 = _s5_prep(lam_re, lam_im, log_dt, b_re, b_im, c_re, c_im, d_skip)
    nc_ctx, nc_lat = CTX_LEN // S5_L, SEQ // S5_L
    n_chunks = nc_ctx + nc_lat
    u = z[:, :D_A]
    ug_ctx = u[N_LAT_ROWS:].reshape(BATCH, nc_ctx, S5_L, S5_G, S5_H).transpose(3, 1, 0, 2, 4)
    ug_ctx = ug_ctx.reshape(S5_G, nc_ctx * BATCH, S5_L * S5_H)
    ug_lat = u[:N_LAT_ROWS].reshape(BATCH, GRID_W // S5_L, S5_L, GRID_ROWS, S5_G, S5_H)
    ug_lat = ug_lat.transpose(4, 3, 1, 0, 2, 5).reshape(S5_G, nc_lat * BATCH, S5_L * S5_H)
    xf, xb = _s5_local(ug_lat, ug_ctx, wst)
    a16_lanes = jnp.swapaxes(a16, 0, 1).reshape(4, 1, S5_G * 2 * S5_P)
    pf, pb = _s5_scan(xf, xb, a16_lanes)
    yg = _s5_out(pf, pb, ug_lat, wc)
    y_lat = yg.reshape(S5_G, GRID_ROWS, GRID_W // S5_L, BATCH, S5_L, S5_H)
    y_lat = y_lat.transpose(3, 2, 4, 1, 0, 5).reshape(N_LAT_ROWS, D_A)
    return _s5_glu(y_lat, w_glu)


HG_PREP_ROWS = 256
HG_UNROLL = 12


def _chunk_cumsum(x, reverse):
    n = x.shape[0]
    pos = lax.broadcasted_iota(jnp.int32, (n, 1), 0) & (HG_CHUNK - 1)
    s = 1
    while s < HG_CHUNK:
        if reverse:
            x = x + jnp.where(pos < HG_CHUNK - s, pltpu.roll(x, n - s, axis=0), 0.0)
        else:
            x = x + jnp.where(pos >= s, pltpu.roll(x, s, axis=0), 0.0)
        s *= 2
    return x


def _hgrn_kernel(qc_ref, ql_ref, vc_ref, vl_ref, ffc_ref, ffl_ref, fbc_ref, fbl_ref, og_ref, lg_ref,
                 gn_ref, o_ref, qe_s, ke_s, qb_s, kd_s, v_s, dec_s, of_s, ob_s, *, layer):
    c = HG_CHUNK
    n_chunks = T_ALL // c
    n_ctx_chunks = CTX_LEN // c
    row = lax.broadcasted_iota(jnp.int32, (c, c), 0)
    col = lax.broadcasted_iota(jnp.int32, (c, c), 1)
    causal = row >= col
    anti = row <= col
    nt = (((1,), (1,)), ((), ()))
    tn = (((0,), (0,)), ((), ()))

    def lower_bound(d):
        logits = [lg_ref[d, j, 0] for j in range(lg_ref.shape[1])]
        top = functools.reduce(jnp.maximum, logits)
        e = [jnp.exp(l - top) for l in logits]
        return sum(e[:layer + 1]) / sum(e)

    lb = (lower_bound(0), lower_bound(1))

    def prepare(row0, chunk0, q, v, fpre):
        n = q.shape[0]
        rows = pl.ds(row0, n)
        as_chunks = lambda a: a.reshape(n // c, c, HG_DK)
        flat_bf16 = lambda a: a.reshape(n, HG_DK).astype(BF16)
        v_s[rows, :] = v.astype(BF16)
        q3 = as_chunks(q)
        for d in range(2):
            f = lb[d] + (1.0 - lb[d]) * jax.nn.sigmoid(fpre[d])
            k3 = as_chunks(1.0 - f)
            b3 = as_chunks(_chunk_cumsum(jnp.log(f), reverse=d == 1))
            mid = c // 2 - 1 if d == 0 else c // 2
            last = c - 1 if d == 0 else 0
            b_mid, b_last = b3[:, mid:mid + 1, :], b3[:, last:last + 1, :]
            qe_s[d, rows, :] = flat_bf16(q3 * jnp.exp(b3 - b_mid))
            ke_s[d, rows, :] = flat_bf16(k3 * jnp.exp(b_mid - b3))
            qb_s[d, rows, :] = flat_bf16(q3 * jnp.exp(b3))
            kd_s[d, rows, :] = flat_bf16(k3 * jnp.exp(b_last - b3))
            dec_s[d, pl.ds(chunk0, n // c)] = jnp.exp(b_last)

    prepare(0, 0, qc_ref[...], vc_ref[...], (ffc_ref[...], fbc_ref[...]))

    def prep_body(blk, carry):
        r0 = pl.multiple_of(blk * HG_PREP_ROWS, HG_PREP_ROWS)
        src = pl.ds(r0, HG_PREP_ROWS)
        prepare(pl.multiple_of(CTX_LEN + r0, HG_PREP_ROWS), n_ctx_chunks + blk * (HG_PREP_ROWS // c),
                ql_ref[src, :], vl_ref[src, :], (ffl_ref[src, :], fbl_ref[src, :]))
        return carry

    lax.fori_loop(0, SEQ // HG_PREP_ROWS, prep_body, 0)

    def chunk(cidx, d, state_t):
        rows = pl.ds(pl.multiple_of(cidx * c, c), c)
        v = v_s[rows, :]
        scores = lax.dot_general(qe_s[d, rows, :], ke_s[d, rows, :], nt, preferred_element_type=F32)
        scores = jnp.where(causal if d == 0 else anti, scores, 0.0)
        o = jnp.dot(scores.astype(BF16), v, preferred_element_type=F32)
        o = o + lax.dot_general(qb_s[d, rows, :], state_t.astype(BF16), nt, preferred_element_type=F32)
        kv_t = lax.dot_general(v, kd_s[d, rows, :], tn, preferred_element_type=F32)
        (of_s if d == 0 else ob_s)[rows, :] = o
        return state_t * dec_s[d, cidx] + kv_t

    def body(i, carry):
        sf, sb = carry
        sf = chunk(i, 0, sf)
        jb = jnp.where(i < n_ctx_chunks, n_ctx_chunks - 1 - i, n_chunks - 1 + n_ctx_chunks - i)
        sb = chunk(jb, 1, sb)
        return sf, sb

    zero = jnp.zeros((HG_DK, HG_DK), F32)
    lax.fori_loop(0, n_chunks, body, (zero, zero), unroll=HG_UNROLL)

    o = of_s[CTX_LEN:, :] + ob_s[CTX_LEN:, :]
    o = o * lax.rsqrt(jnp.mean(o * o, axis=-1, keepdims=True) + EPS)
    o_ref[...] = (o * gn_ref[0] * jax.nn.silu(og_ref[...])).astype(o_ref.dtype)


def _hgrn_branch(z, lb_logits, hg_norm, layer):
    n_slots = lb_logits.shape[1]
    ctx_blk0 = N_LAT_ROWS // CTX_LEN

    def windows(i):
        col_blk = (D_A + i * D_B) // HG_DK
        return [pl.BlockSpec((CTX_LEN, HG_DK), lambda b, h: (ctx_blk0 + b, col_blk + h)),
                pl.BlockSpec((SEQ, HG_DK), lambda b, h: (b, col_blk + h))]

    in_specs = [spec for i in range(4) for spec in windows(i)]
    in_specs += [windows(4)[1],
                 pl.BlockSpec((2, n_slots, 1, 1, HG_DK), lambda b, h: (0, 0, h, 0, 0)),
                 pl.BlockSpec((1, 1, HG_DK), lambda b, h: (h, 0, 0))]
    operand = lambda: pltpu.VMEM((2, T_ALL, HG_DK), BF16)
    lg = lb_logits.astype(F32).reshape(2, n_slots, HG_HEADS, 1, HG_DK)
    return pl.pallas_call(
        functools.partial(_hgrn_kernel, layer=layer),
        grid=(BATCH, HG_HEADS),
        in_specs=in_specs,
        out_specs=pl.BlockSpec((SEQ, HG_DK), lambda b, h: (b, h)),
        out_shape=jax.ShapeDtypeStruct((N_LAT_ROWS, D_B), BF16),
        scratch_shapes=[operand(), operand(), operand(), operand(),
                        pltpu.VMEM((T_ALL, HG_DK), BF16),
                        pltpu.VMEM((2, T_ALL // HG_CHUNK, 1, HG_DK), F32),
                        pltpu.VMEM((T_ALL, HG_DK), F32),
                        pltpu.VMEM((T_ALL, HG_DK), F32)],
        compiler_params=_params(("parallel", "parallel"), 40),
        name="hgrn2",
    )(*([z] * 9), lg, hg_norm.reshape(HG_HEADS, 1, HG_DK))


FF_DOWN_TILE = 512


def kernel(x, c, ctx, c_ctx, w_ada, b_ada, norm_ffn1, w1_ffn1, w3_ffn1, w2_ffn1, norm_mix, w_in,
           s5_lam_re, s5_lam_im, s5_log_dt, s5_b_re, s5_b_im, s5_c_re, s5_c_im, s5_d, s5_w_glu,
           hg_lb_logits, hg_norm, w_proj_a, w_proj_b, w_out, norm_ffn2, w1_ffn2, w3_ffn2, w2_ffn2,
           norm_final):
    layer = 0
    c8 = jnp.concatenate([c, c_ctx[None], jnp.zeros((8 - BATCH - 1, D_MODEL), F32)], axis=0)
    mod = _ada(c8, w_ada[layer], b_ada[layer]).reshape(8, N_MOD, D_MODEL)
    mod = jnp.concatenate([mod[BATCH:BATCH + 1], mod[:BATCH]], axis=0)
    m = [mod[:, i].reshape(BATCH + 1, 1, D_MODEL) for i in range(N_MOD)]

    tile = 1024

    z1, h0 = _entry(x, ctx, norm_ffn1[layer], m[0], m[1])
    g1, w2_bf16 = _swiglu_up(z1, w1_ffn1[layer], w3_ffn1[layer], w2_ffn1[layer])
    h1 = _mm_res(g1, w2_bf16, h0, 0, m[2], 0.5, _group_full(FF_DOWN_TILE),
                 FF_DOWN_TILE, FF_DOWN_TILE, D_FF, 56)

    z2 = _norm_mod(h1, norm_mix[layer], m[3], m[4], _group_full(512))
    z = _mm(z2, w_in[layer], F32, tile, 512, 52, "mixer_in_proj")

    y_a = _s5_branch(z, s5_lam_re[layer], s5_lam_im[layer], s5_log_dt[layer], s5_b_re[layer],
                     s5_b_im[layer], s5_c_re[layer], s5_c_im[layer], s5_d[layer], s5_w_glu[layer])
    y_b = _hgrn_branch(z, hg_lb_logits, hg_norm[layer], layer)

    ga_tile = (D_A + 5 * D_B) // 512
    merged = _merge(y_a, w_proj_a[layer], y_b, w_proj_b[layer], z,
                    ga_tile, ga_tile + D_MODEL // 512, 0)
    h2 = _mm_res(merged, w_out[layer], h1, 0, m[5], 1.0, _group_lat(tile),
                 tile, 512, D_MODEL, 52)

    z3 = _norm_mod(h2, norm_ffn2[layer], m[6], m[7], _group_lat(512))
    g2, w2_bf16 = _swiglu_up(z3, w1_ffn2[layer], w3_ffn2[layer], w2_ffn2[layer])
    h3 = _mm_res(g2, w2_bf16, h2, 0, m[8], 0.5, _group_lat(FF_DOWN_TILE),
                 FF_DOWN_TILE, FF_DOWN_TILE, D_FF, 56)
    return _final_norm(h3, norm_final).reshape(BATCH, SEQ, D_MODEL)
```
